```python
import math
import jax, jax.numpy as jnp
from jax import lax
import numpy as np

D_MODEL = 2048
BATCH = 4
SEQ = 4096
DEPTH = 4

N_EVEN = (DEPTH + 1) // 2
N_ODD = DEPTH // 2

CONV_CH = D_MODEL // 2
CONV_WIDTH = 31
SGU_CH = D_MODEL // 2
SGU_GROUPS = 8
SGU_GROUP_DIM = SGU_CH // SGU_GROUPS
CHUNK = 128
IN_AB = 2 * CONV_CH + 2 * SGU_CH
C_HEADS = 8
C_HEAD_DIM = 128
QK_W = C_HEADS * 2 * C_HEAD_DIM
V_W = C_HEADS * 2 * C_HEAD_DIM
Q_BLOCK = 128
NUM_BUCKETS = 32
MAX_DISTANCE = 128
D_FF = ((8 * D_MODEL // 3 + 255) // 256) * 256
EPS = 1e-6

kernel_name = "hybrid_conv_sgu_diffattn_encoder"


def rms_norm(x, g):
    xf = x.astype(jnp.float32)
    y = xf * lax.rsqrt(jnp.mean(xf * xf, axis=-1, keepdims=True) + EPS)
    return (y * g.astype(jnp.float32)).astype(x.dtype)


def layer_norm(x, g, b):
    xf = x.astype(jnp.float32)
    mu = jnp.mean(xf, axis=-1, keepdims=True)
    xc = xf - mu
    y = xc * lax.rsqrt(jnp.mean(xc * xc, axis=-1, keepdims=True) + EPS)
    return (y * g.astype(jnp.float32) + b.astype(jnp.float32)).astype(x.dtype)


def t5_bucket(rel):
    nb = NUM_BUCKETS // 2
    max_exact = nb // 2
    ret = jnp.where(rel > 0, nb, 0)
    n = jnp.abs(rel)
    nf = jnp.maximum(n, 1).astype(jnp.float32)
    large = max_exact + (jnp.log(nf / max_exact) / math.log(MAX_DISTANCE / max_exact)
                         * (nb - max_exact)).astype(jnp.int32)
    large = jnp.minimum(large, nb - 1)
    return ret + jnp.where(n < max_exact, n, large)


def conv_sgu_mixer(h, w_in, dw_k, cg, cb, sg, sb, w_sp, b_sp, w_out):
    B_, S_, _ = h.shape
    z = h @ w_in
    a_lin, a_gate, u, v = jnp.split(z, [CONV_CH, 2 * CONV_CH, 2 * CONV_CH + SGU_CH], axis=-1)
    a = a_lin * jax.nn.sigmoid(a_gate)
    pad = CONV_WIDTH // 2
    a = lax.conv_general_dilated(a, dw_k[:, None, :].astype(a.dtype), window_strides=(1,),
                                 padding=[(pad, pad)], dimension_numbers=('NWC', 'WIO', 'NWC'),
                                 feature_group_count=CONV_CH)
    a = jax.nn.silu(layer_norm(a, cg, cb))
    zu = jax.nn.gelu(u, approximate=False)
    zv = layer_norm(jax.nn.gelu(v, approximate=False), sg, sb)
    n_chunks = S_ // CHUNK
    zv = zv.reshape(B_, n_chunks, CHUNK, SGU_GROUPS, SGU_GROUP_DIM)
    s = jnp.einsum('gpq,bnqgc->bnpgc', w_sp, zv) + b_sp.T[:, :, None]
    bo = zu * s.reshape(B_, S_, SGU_CH)
    return jnp.concatenate([a, bo], axis=-1) @ w_out


def diff_attention(h, w_qkv, lam_qk, subln_g, w_out, rel_bias, lambda_init):
    B_, S_, _ = h.shape
    qkv = h @ w_qkv
    q, k, v = jnp.split(qkv, [QK_W, 2 * QK_W], axis=-1)
    q = q.reshape(B_, S_, C_HEADS, 2, C_HEAD_DIM) * (C_HEAD_DIM ** -0.5)
    k = k.reshape(B_, S_, C_HEADS, 2, C_HEAD_DIM)
    v = v.reshape(B_, S_, C_HEADS, 2 * C_HEAD_DIM)
    lf = lam_qk.astype(jnp.float32)
    lam = jnp.exp(jnp.sum(lf[0] * lf[1])) - jnp.exp(jnp.sum(lf[2] * lf[3])) + lambda_init
    nblk = S_ // Q_BLOCK
    qb = q.reshape(B_, nblk, Q_BLOCK, C_HEADS, 2, C_HEAD_DIM).transpose(1, 0, 2, 3, 4, 5)
    k_pos = jnp.arange(S_, dtype=jnp.int32)

    def block(args):
        q_blk, i = args
        q_pos = i * Q_BLOCK + jnp.arange(Q_BLOCK, dtype=jnp.int32)
        bucket = t5_bucket(k_pos[None, :] - q_pos[:, None])
        bias = jnp.take(rel_bias, bucket, axis=0).transpose(2, 0, 1)
        logits = (jnp.einsum('bqhcd,bkhcd->bhcqk', q_blk, k).astype(jnp.float32)
                  + bias[None, :, None].astype(jnp.float32))
        p = jax.nn.softmax(logits, axis=-1)
        attn = (p[:, :, 0] - lam * p[:, :, 1]).astype(v.dtype)
        return jnp.einsum('bhqk,bkhe->bqhe', attn, v)

    o = lax.map(block, (qb, jnp.arange(nblk, dtype=jnp.int32)))
    o = o.transpose(1, 0, 2, 3, 4).reshape(B_, S_, C_HEADS, 2 * C_HEAD_DIM)
    o = rms_norm(o, subln_g) * (1.0 - lambda_init)
    return o.reshape(B_, S_, V_W) @ w_out


def swiglu(h, wg, wu, wd):
    return (jax.nn.silu(h @ wg) * (h @ wu)) @ wd


def setup_inputs(seed: int = 0) -> dict:
    key = jax.random.key(seed)
    ks = jax.random.split(key, 20)
    f32 = jnp.float32
    nrm = lambda k, shape, fan: jax.random.normal(k, shape, f32) * (fan ** -0.5)
    return {
        "x": jax.random.normal(ks[0], (BATCH, SEQ, D_MODEL), f32),
        "norm_g": 1.0 + 0.05 * jax.random.normal(ks[1], (DEPTH, 4, D_MODEL), f32),
        "w_in_ab": nrm(ks[2], (N_EVEN, D_MODEL, IN_AB), D_MODEL),
        "dw_kernel": nrm(ks[3], (N_EVEN, CONV_WIDTH, CONV_CH), CONV_WIDTH),
        "conv_ln_g": 1.0 + 0.05 * jax.random.normal(ks[4], (N_EVEN, CONV_CH), f32),
        "conv_ln_b": 0.02 * jax.random.normal(ks[5], (N_EVEN, CONV_CH), f32),
        "sgu_ln_g": 1.0 + 0.05 * jax.random.normal(ks[6], (N_EVEN, SGU_CH), f32),
        "sgu_ln_b": 0.02 * jax.random.normal(ks[7], (N_EVEN, SGU_CH), f32),
        "w_spatial": nrm(ks[8], (N_EVEN, SGU_GROUPS, CHUNK, CHUNK), CHUNK),
        "b_spatial": 1.0 + 0.05 * jax.random.normal(ks[9], (N_EVEN, SGU_GROUPS, CHUNK), f32),
        "w_out_ab": nrm(ks[10], (N_EVEN, CONV_CH + SGU_CH, D_MODEL), CONV_CH + SGU_CH),
        "w_qkv_c": nrm(ks[11], (N_ODD, D_MODEL, 2 * QK_W + V_W), D_MODEL),
        "lambda_qk": 0.1 * jax.random.normal(ks[12], (N_ODD, 4, C_HEAD_DIM), f32),
        "subln_g": 1.0 + 0.05 * jax.random.normal(ks[13], (N_ODD, 2 * C_HEAD_DIM), f32),
        "w_out_c": nrm(ks[14], (N_ODD, V_W, D_MODEL), V_W),
        "rel_bias": 0.5 * jax.random.normal(ks[15], (NUM_BUCKETS, C_HEADS), f32),
        "w_gate": nrm(ks[16], (DEPTH, D_MODEL, D_FF), D_MODEL),
        "w_up": nrm(ks[17], (DEPTH, D_MODEL, D_FF), D_MODEL),
        "w_down": nrm(ks[18], (DEPTH, D_FF, D_MODEL), D_FF),
    }


def reference(x, norm_g, w_in_ab, dw_kernel, conv_ln_g, conv_ln_b, sgu_ln_g, sgu_ln_b,
              w_spatial, b_spatial, w_out_ab, w_qkv_c, lambda_qk, subln_g, w_out_c,
              rel_bias, w_gate, w_up, w_down):
    for l in range(DEPTH):
        h = rms_norm(x, norm_g[l, 0])
        if l % 2 == 0:
            e = l // 2
            m = conv_sgu_mixer(h, w_in_ab[e], dw_kernel[e], conv_ln_g[e], conv_ln_b[e],
                               sgu_ln_g[e], sgu_ln_b[e], w_spatial[e], b_spatial[e], w_out_ab[e])
        else:
            o = l // 2
            lambda_init = 0.8 - 0.6 * math.exp(-0.3 * l)
            m = diff_attention(h, w_qkv_c[o], lambda_qk[o], subln_g[o], w_out_c[o],
                               rel_bias, lambda_init)
        x = x + rms_norm(m, norm_g[l, 1])
        h = rms_norm(x, norm_g[l, 2])
        x = x + rms_norm(swiglu(h, w_gate[l], w_up[l], w_down[l]), norm_g[l, 3])
    return x
```

```python
import functools
import math

import jax
import jax.numpy as jnp
from jax import lax
from jax.experimental import pallas as pl
from jax.experimental.pallas import tpu as pltpu

F32 = jnp.float32
BF16 = jnp.bfloat16

EPS = 1e-6
CONV_WIDTH = 31
CHUNK = 128
SGU_GROUPS = 8
C_HEADS = 8
C_HEAD_DIM = 128
NUM_BUCKETS = 32
MAX_DISTANCE = 128

VMEM_LIMIT_BYTES = 56 * 1024 * 1024
HALO = 16

TM = 512
TF = 512
TS = 256
TQ = 256
CONV_ROWS = 32


def _params(grid_rank):
    return pltpu.CompilerParams(dimension_semantics=("arbitrary",) * grid_rank,
                                vmem_limit_bytes=VMEM_LIMIT_BYTES)


def _rms(v, g):
    return (v * lax.rsqrt(jnp.mean(v * v, axis=-1, keepdims=True) + EPS)) * g


def _layer_norm(v, g, b):
    mu = jnp.mean(v, axis=-1, keepdims=True)
    vc = v - mu
    return (vc * lax.rsqrt(jnp.mean(vc * vc, axis=-1, keepdims=True) + EPS)) * g + b


def _gelu(v):
    return 0.5 * v * (1.0 + lax.erf(v * math.sqrt(0.5)))


def _silu(v):
    return v * jax.nn.sigmoid(v)


def _residual_epilogue(m, x_ref, g_post_ref, g_next_ref, xo_ref, ho_ref):
    xn = x_ref[...] + _rms(m, g_post_ref[...])
    xo_ref[...] = xn
    ho_ref[...] = _rms(xn, g_next_ref[...]).astype(ho_ref.dtype)


def _prenorm_kernel(x_ref, g_ref, h_ref):
    h_ref[...] = _rms(x_ref[...], g_ref[...]).astype(h_ref.dtype)


def _prenorm(x2, g):
    n, d = x2.shape
    return pl.pallas_call(
        _prenorm_kernel,
        grid=(n // TM,),
        in_specs=[pl.BlockSpec((TM, d), lambda i: (i, 0)),
                  pl.BlockSpec((1, d), lambda i: (0, 0))],
        out_specs=pl.BlockSpec((TM, d), lambda i: (i, 0)),
        out_shape=jax.ShapeDtypeStruct((n, d), BF16),
        compiler_params=_params(1),
        name="prenorm",
    )(x2, g)


def _glu_in_kernel(h_ref, w_ref, a_ref):
    z = jnp.dot(h_ref[...], w_ref[...], preferred_element_type=F32)
    c = a_ref.shape[-1]
    a_ref[...] = z[:, :c] * jax.nn.sigmoid(z[:, c:])


def _sgu_in_kernel(h_ref, w_ref, g_ref, b_ref, zu_ref, zv_ref):
    z = jnp.dot(h_ref[...], w_ref[...], preferred_element_type=F32)
    c = zu_ref.shape[-1]
    zu_ref[...] = _gelu(z[:, :c]).astype(zu_ref.dtype)
    zv_ref[...] = _layer_norm(_gelu(z[:, c:]), g_ref[...], b_ref[...]).astype(zv_ref.dtype)


def _in_projection(h, w_in, e, sg, sb):
    n, d = h.shape
    c = w_in.shape[-1] // 4
    h_spec = pl.BlockSpec((TM, d), lambda i: (i, 0))
    vec_spec = pl.BlockSpec((1, c), lambda i: (0, 0))
    out_spec = pl.BlockSpec((TM, c), lambda i: (i, 0))
    a = pl.pallas_call(
        _glu_in_kernel,
        grid=(n // TM,),
        in_specs=[h_spec, pl.BlockSpec((None, d, 2 * c), lambda i: (e, 0, 0))],
        out_specs=out_spec,
        out_shape=jax.ShapeDtypeStruct((n, c), F32),
        compiler_params=_params(1),
        name="glu_in",
    )(h, w_in)
    zu, zv = pl.pallas_call(
        _sgu_in_kernel,
        grid=(n // TM,),
        in_specs=[h_spec, pl.BlockSpec((None, d, 2 * c), lambda i: (e, 0, 1)), vec_spec, vec_spec],
        out_specs=[out_spec, out_spec],
        out_shape=[jax.ShapeDtypeStruct((n, c), BF16)] * 2,
        compiler_params=_params(1),
        name="sgu_in",
    )(h, w_in, sg, sb)
    return a, zu, zv


def _even_mix_kernel(ap_ref, a_ref, an_ref, zu_ref, zv_ref, x_ref, dw_ref, cg_ref, cb_ref,
                     wsp_ref, bsp_ref, wo_ref, g_post_ref, g_next_ref,
                     xo_ref, ho_ref, win_ref, conv_ref, cat_ref):
    i = pl.program_id(1)
    ts, c = a_ref.shape
    win_ref[0:HALO, :] = jnp.where(i > 0, ap_ref[...], 0.0)
    win_ref[HALO:HALO + ts, :] = a_ref[...]
    win_ref[HALO + ts:, :] = jnp.where(i < pl.num_programs(1) - 1, an_ref[...], 0.0)
    first = HALO - CONV_WIDTH // 2
    for r0 in range(0, ts, CONV_ROWS):
        acc = jnp.zeros((CONV_ROWS, c), F32)
        for w in range(CONV_WIDTH):
            acc = acc + win_ref[r0 + first + w:r0 + first + w + CONV_ROWS, :] * dw_ref[w:w + 1, :]
        conv_ref[r0:r0 + CONV_ROWS, :] = acc
    cat_ref[:, :c] = _silu(_layer_norm(conv_ref[...], cg_ref[...], cb_ref[...])).astype(cat_ref.dtype)
    gd = c // SGU_GROUPS
    for r0 in range(0, ts, CHUNK):
        for g in range(SGU_GROUPS):
            cols = slice(g * gd, (g + 1) * gd)
            s = jnp.dot(wsp_ref[g], zv_ref[r0:r0 + CHUNK, cols], preferred_element_type=F32)
            s = s + bsp_ref[:, cols]
            gated = zu_ref[r0:r0 + CHUNK, cols].astype(F32) * s
            cat_ref[r0:r0 + CHUNK, c + g * gd:c + (g + 1) * gd] = gated.astype(cat_ref.dtype)
    m = jnp.dot(cat_ref[...], wo_ref[...], preferred_element_type=F32)
    _residual_epilogue(m, x_ref, g_post_ref, g_next_ref, xo_ref, ho_ref)


def _even_mix(a, zu, zv, x, dw, cg, cb, wsp, bsp, w_out, e, g_post, g_next):
    b, s, c = a.shape
    d = x.shape[-1]
    nblk = TS // HALO
    tile = lambda width: pl.BlockSpec((None, TS, width), lambda bi, i: (bi, i, 0))
    const2 = lambda shape: pl.BlockSpec(shape, lambda bi, i: (0, 0))
    return pl.pallas_call(
        _even_mix_kernel,
        grid=(b, s // TS),
        in_specs=[
            pl.BlockSpec((None, HALO, c), lambda bi, i: (bi, jnp.maximum(i * nblk - 1, 0), 0)),
            tile(c),
            pl.BlockSpec((None, HALO, c),
                         lambda bi, i: (bi, jnp.minimum((i + 1) * nblk, s // HALO - 1), 0)),
            tile(c), tile(c), tile(d),
            const2((CONV_WIDTH, c)), const2((1, c)), const2((1, c)),
            pl.BlockSpec((SGU_GROUPS, CHUNK, CHUNK), lambda bi, i: (0, 0, 0)),
            const2((CHUNK, c)),
            pl.BlockSpec((None, 2 * c, d), lambda bi, i: (e, 0, 0)),
            const2((1, d)), const2((1, d)),
        ],
        out_specs=[tile(d), tile(d)],
        out_shape=[jax.ShapeDtypeStruct((b, s, d), F32), jax.ShapeDtypeStruct((b, s, d), BF16)],
        scratch_shapes=[pltpu.VMEM((TS + 2 * HALO, c), F32), pltpu.VMEM((TS, c), F32),
                        pltpu.VMEM((TS, 2 * c), BF16)],
        compiler_params=_params(2),
        name="even_mix",
    )(a, a, a, zu, zv, x, dw, cg, cb, wsp, bsp, w_out, g_post, g_next)


def _qkv_kernel(h_ref, w_ref, o_ref, *, q_scale):
    z = jnp.dot(h_ref[...], w_ref[...], preferred_element_type=F32)
    scale = jnp.where(pl.program_id(0) == 0, q_scale, 1.0)
    o_ref[...] = (z * scale).astype(o_ref.dtype)


def _qkv_projection(h, w_qkv, o):
    n, d = h.shape
    width = w_qkv.shape[-1] // 3
    return pl.pallas_call(
        functools.partial(_qkv_kernel, q_scale=C_HEAD_DIM ** -0.5),
        grid=(3, n // TM),
        in_specs=[pl.BlockSpec((TM, d), lambda j, i: (i, 0)),
                  pl.BlockSpec((None, d, width), lambda j, i: (o, 0, j))],
        out_specs=pl.BlockSpec((TM, width), lambda j, i: (i, j)),
        out_shape=jax.ShapeDtypeStruct((n, 3 * width), BF16),
        compiler_params=_params(2),
        name="qkv_proj",
    )(h, w_qkv)


def _t5_bucket(rel):
    nb = NUM_BUCKETS // 2
    max_exact = nb // 2
    ret = jnp.where(rel > 0, nb, 0)
    n = jnp.abs(rel)
    nf = jnp.maximum(n, 1).astype(jnp.float32)
    large = max_exact + (jnp.log(nf / max_exact) / math.log(MAX_DISTANCE / max_exact)
                         * (nb - max_exact)).astype(jnp.int32)
    large = jnp.minimum(large, nb - 1)
    return ret + jnp.where(n < max_exact, n, large)


def _bias_band_kernel(bucket_ref, rb_ref, band_ref):
    h = pl.program_id(0)
    t = band_ref.shape[0]
    bucket = bucket_ref[...]
    row = jnp.zeros(bucket.shape, F32)
    for bkt in range(NUM_BUCKETS):
        row = jnp.where(bucket == bkt, rb_ref[bkt, h], row)
    rolled = pltpu.roll(jnp.broadcast_to(row, (t, 4 * t)), 0, 1, stride=1, stride_axis=0)
    band_ref[...] = rolled[:, t:]


def _bias_band(rel_bias):
    t = TQ
    assert t >= MAX_DISTANCE
    bucket = _t5_bucket(jnp.arange(-2 * t, 2 * t, dtype=jnp.int32))[None, :]
    return pl.pallas_call(
        _bias_band_kernel,
        grid=(C_HEADS,),
        in_specs=[pl.BlockSpec((1, 4 * t), lambda h: (0, 0)),
                  pl.BlockSpec(memory_space=pltpu.SMEM)],
        out_specs=pl.BlockSpec((None, t, 3 * t), lambda h: (h, 0, 0)),
        out_shape=jax.ShapeDtypeStruct((C_HEADS, t, 3 * t), F32),
        compiler_params=_params(1),
        name="bias_band",
    )(bucket, rel_bias)


def _diff_attn_kernel(rb_ref, q_ref, k_ref, v_ref, band_ref, lam_ref, sg_ref, o_ref,
                      m_ref, l_ref, acc_ref, *, lambda_init):
    h = pl.program_id(1)
    qi = pl.program_id(2)
    t, dh2 = q_ref.shape
    dh = dh2 // 2
    nk = k_ref.shape[0] // t
    nb = NUM_BUCKETS // 2
    c_left = rb_ref[nb - 1, h]
    c_right = rb_ref[2 * nb - 1, h]

    m_ref[...] = jnp.full(m_ref.shape, -jnp.inf, F32)
    l_ref[...] = jnp.zeros(l_ref.shape, F32)
    acc_ref[...] = jnp.zeros(acc_ref.shape, F32)

    def chunk(kj, bias_tile, bias_const):
        rows = pl.ds(pl.multiple_of(kj * t, t), t)
        v = v_ref[rows, :]
        for c in range(2):
            q = q_ref[:, c * dh:(c + 1) * dh]
            k = k_ref[rows, c * dh:(c + 1) * dh]
            s = lax.dot_general(q, k, (((1,), (1,)), ((), ())), preferred_element_type=F32)
            if bias_tile is not None:
                s = s + bias_tile
            m_prev = m_ref[c]
            m_new = jnp.maximum(m_prev, jnp.max(s, axis=1, keepdims=True) + bias_const)
            alpha = jnp.exp(m_prev - m_new)
            p = jnp.exp(s - (m_new - bias_const))
            l_ref[c] = alpha * l_ref[c] + jnp.sum(p, axis=1, keepdims=True)
            acc_ref[c] = alpha * acc_ref[c] + jnp.dot(p.astype(v.dtype), v,
                                                      preferred_element_type=F32)
            m_ref[c] = m_new

    def left_body(kj, carry):
        chunk(kj, None, c_left)
        return carry

    def right_body(kj, carry):
        chunk(kj, None, c_right)
        return carry

    lax.fori_loop(0, jnp.maximum(qi - 1, 0), left_body, 0)

    @pl.when(qi > 0)
    def _():
        chunk(qi - 1, band_ref[:, 0:t], 0.0)

    chunk(qi, band_ref[:, t:2 * t], 0.0)

    @pl.when(qi < nk - 1)
    def _():
        chunk(qi + 1, band_ref[:, 2 * t:3 * t], 0.0)

    lax.fori_loop(jnp.minimum(qi + 2, nk), nk, right_body, 0)

    lq = lam_ref[...]
    lam = (jnp.exp(jnp.sum(lq[0:1] * lq[1:2], axis=1, keepdims=True))
           - jnp.exp(jnp.sum(lq[2:3] * lq[3:4], axis=1, keepdims=True)) + lambda_init)
    o = acc_ref[0] / l_ref[0] - lam * (acc_ref[1] / l_ref[1])
    o_ref[...] = (_rms(o, sg_ref[...]) * (1.0 - lambda_init)).astype(o_ref.dtype)


def _diff_attention(qkv, band, rel_bias, lam_qk, subln_g, lambda_init):
    b, s, _ = qkv.shape
    dh2 = 2 * C_HEAD_DIM
    t = TQ
    return pl.pallas_call(
        functools.partial(_diff_attn_kernel, lambda_init=lambda_init),
        grid=(b, C_HEADS, s // t),
        in_specs=[
            pl.BlockSpec(memory_space=pltpu.SMEM),
            pl.BlockSpec((None, t, dh2), lambda bi, h, qi: (bi, qi, h)),
            pl.BlockSpec((None, s, dh2), lambda bi, h, qi: (bi, 0, C_HEADS + h)),
            pl.BlockSpec((None, s, dh2), lambda bi, h, qi: (bi, 0, 2 * C_HEADS + h)),
            pl.BlockSpec((None, t, 3 * t), lambda bi, h, qi: (h, 0, 0)),
            pl.BlockSpec((4, C_HEAD_DIM), lambda bi, h, qi: (0, 0)),
            pl.BlockSpec((1, dh2), lambda bi, h, qi: (0, 0)),
        ],
        out_specs=pl.BlockSpec((None, t, dh2), lambda bi, h, qi: (bi, qi, h)),
        out_shape=jax.ShapeDtypeStruct((b, s, C_HEADS * dh2), BF16),
        scratch_shapes=[pltpu.VMEM((2, t, 1), F32), pltpu.VMEM((2, t, 1), F32),
                        pltpu.VMEM((2, t, dh2), F32)],
        compiler_params=_params(3),
        name="diff_attn",
    )(rel_bias, qkv, qkv, qkv, band, lam_qk, subln_g)


def _out_proj_kernel(a_ref, w_ref, x_ref, g_post_ref, g_next_ref, xo_ref, ho_ref):
    m = jnp.dot(a_ref[...], w_ref[...], preferred_element_type=F32)
    _residual_epilogue(m, x_ref, g_post_ref, g_next_ref, xo_ref, ho_ref)


def _out_projection(a, w, o, x, g_post, g_next):
    n, k = a.shape
    d = x.shape[-1]
    row = lambda width: pl.BlockSpec((TM, width), lambda i: (i, 0))
    vec = pl.BlockSpec((1, d), lambda i: (0, 0))
    return pl.pallas_call(
        _out_proj_kernel,
        grid=(n // TM,),
        in_specs=[row(k), pl.BlockSpec((None, k, d), lambda i: (o, 0, 0)), row(d), vec, vec],
        out_specs=[row(d), row(d)],
        out_shape=[jax.ShapeDtypeStruct((n, d), F32), jax.ShapeDtypeStruct((n, d), BF16)],
        compiler_params=_params(1),
        name="out_proj",
    )(a, w, x, g_post, g_next)


def _ffn_kernel(h_ref, wg_ref, wu_ref, wd_ref, x_ref, g_post_ref, g_next_ref,
                xo_ref, ho_ref, acc_ref):
    f = pl.program_id(1)
    h = h_ref[...]
    gate = jnp.dot(h, wg_ref[...], preferred_element_type=F32)
    up = jnp.dot(h, wu_ref[...], preferred_element_type=F32)
    act = (_silu(gate) * up).astype(h.dtype)
    part = jnp.dot(act, wd_ref[...], preferred_element_type=F32)

    @pl.when(f == 0)
    def _():
        acc_ref[...] = part

    @pl.when(f > 0)
    def _():
        acc_ref[...] += part

    @pl.when(f == pl.num_programs(1) - 1)
    def _():
        _residual_epilogue(acc_ref[...], x_ref, g_post_ref, g_next_ref, xo_ref, ho_ref)


def _ffn(h, wg, wu, wd, layer, x, g_post, g_next):
    n, d = h.shape
    ff = wg.shape[-1]
    row = lambda width: pl.BlockSpec((TM, width), lambda i, f: (i, 0))
    vec = pl.BlockSpec((1, d), lambda i, f: (0, 0))
    return pl.pallas_call(
        _ffn_kernel,
        grid=(n // TM, ff // TF),
        in_specs=[row(d),
                  pl.BlockSpec((None, d, TF), lambda i, f: (layer, 0, f)),
                  pl.BlockSpec((None, d, TF), lambda i, f: (layer, 0, f)),
                  pl.BlockSpec((None, TF, d), lambda i, f: (layer, f, 0)),
                  row(d), vec, vec],
        out_specs=[row(d), row(d)],
        out_shape=[jax.ShapeDtypeStruct((n, d), F32), jax.ShapeDtypeStruct((n, d), BF16)],
        scratch_shapes=[pltpu.VMEM((TM, d), F32)],
        compiler_params=_params(2),
        name="ffn",
    )(h, wg, wu, wd, x, g_post, g_next)


def kernel(x, norm_g, w_in_ab, dw_kernel, conv_ln_g, conv_ln_b, sgu_ln_g, sgu_ln_b, w_spatial,
           b_spatial, w_out_ab, w_qkv_c, lambda_qk, subln_g, w_out_c, rel_bias, w_gate, w_up,
           w_down):
    b, s, d = x.shape
    n = b * s
    depth = norm_g.shape[0]
    c = conv_ln_g.shape[-1]
    gd = c // SGU_GROUPS

    w_in = w_in_ab.astype(BF16)
    w_out_e = w_out_ab.astype(BF16)
    w_qkv = w_qkv_c.astype(BF16)
    w_out_o = w_out_c.astype(BF16)
    wsp = w_spatial.astype(BF16)
    wg, wu, wd = w_gate.astype(BF16), w_up.astype(BF16), w_down.astype(BF16)
    band = _bias_band(rel_bias)

    x2 = x.reshape(n, d)
    h = _prenorm(x2, norm_g[0, 0][None, :])
    for layer in range(depth):
        g_post = norm_g[layer, 1][None, :]
        g_ffn = norm_g[layer, 2][None, :]
        if layer % 2 == 0:
            e = layer // 2
            a, zu, zv = _in_projection(h, w_in, e, sgu_ln_g[e][None, :], sgu_ln_b[e][None, :])
            bsp = jnp.repeat(b_spatial[e].T, gd, axis=1)
            x3, h3 = _even_mix(a.reshape(b, s, c), zu.reshape(b, s, c), zv.reshape(b, s, c),
                               x2.reshape(b, s, d), dw_kernel[e], conv_ln_g[e][None, :],
                               conv_ln_b[e][None, :], wsp[e], bsp, w_out_e, e, g_post, g_ffn)
            x2, h = x3.reshape(n, d), h3.reshape(n, d)
        else:
            o = layer // 2
            lambda_init = 0.8 - 0.6 * math.exp(-0.3 * layer)
            qkv = _qkv_projection(h, w_qkv, o)
            att = _diff_attention(qkv.reshape(b, s, -1), band, rel_bias, lambda_qk[o],
                                  subln_g[o][None, :], lambda_init)
            x2, h = _out_projection(att.reshape(n, -1), w_out_o, o, x2, g_post, g_ffn)
        g_next = norm_g[(layer + 1) % depth, 0][None, :]
        x2, h = _ffn(h, wg, wu, wd, layer, x2, norm_g[layer, 3][None, :], g_next)
    return x2.reshape(b, s, d)
```

```python
import functools
import math

import jax
import jax.numpy as jnp
from jax import lax
from jax.experimental import pallas as pl
from jax.experimental.pallas import tpu as pltpu

F32 = jnp.float32
BF16 = jnp.bfloat16

EPS = 1e-6
CONV_WIDTH = 31
CHUNK = 128
SGU_GROUPS = 8
C_HEADS = 8
C_HEAD_DIM = 128
NUM_BUCKETS = 32
MAX_DISTANCE = 128

VMEM_LIMIT_BYTES = 56 * 1024 * 1024
HALO = 16

TM = 512
TF = 512
TS = 256
TQ = 256
CONV_ROWS = 32


def _params(grid_rank):
    return pltpu.CompilerParams(dimension_semantics=("arbitrary",) * grid_rank,
                                vmem_limit_bytes=VMEM_LIMIT_BYTES)


def _rms(v, g):
    return (v * lax.rsqrt(jnp.mean(v * v, axis=-1, keepdims=True) + EPS)) * g


def _layer_norm(v, g, b):
    mu = jnp.mean(v, axis=-1, keepdims=True)
    vc = v - mu
    return (vc * lax.rsqrt(jnp.mean(vc * vc, axis=-1, keepdims=True) + EPS)) * g + b


def _gelu(v):
    return 0.5 * v * (1.0 + lax.erf(v * math.sqrt(0.5)))


def _silu(v):
    return v * jax.nn.sigmoid(v)


def _residual_epilogue(m, x_ref, g_post_ref, g_next_ref, xo_ref, ho_ref):
    xn = x_ref[...] + _rms(m, g_post_ref[...])
    xo_ref[...] = xn
    ho_ref[...] = _rms(xn, g_next_ref[...]).astype(ho_ref.dtype)


def _prenorm_kernel(x_ref, g_ref, h_ref):
    h_ref[...] = _rms(x_ref[...], g_ref[...]).astype(h_ref.dtype)


def _prenorm(x2, g):
    n, d = x2.shape
    return pl.pallas_call(
        _prenorm_kernel,
        grid=(n // TM,),
        in_specs=[pl.BlockSpec((TM, d), lambda i: (i, 0)),
                  pl.BlockSpec((1, d), lambda i: (0, 0))],
        out_specs=pl.BlockSpec((TM, d), lambda i: (i, 0)),
        out_shape=jax.ShapeDtypeStruct((n, d), BF16),
        compiler_params=_params(1),
        name="prenorm",
    )(x2, g)


def _glu_in_kernel(h_ref, w_ref, a_ref):
    z = jnp.dot(h_ref[...], w_ref[...], preferred_element_type=F32)
    c = a_ref.shape[-1]
    a_ref[...] = z[:, :c] * jax.nn.sigmoid(z[:, c:])


def _sgu_in_kernel(h_ref, w_ref, g_ref, b_ref, zu_ref, zv_ref):
    z = jnp.dot(h_ref[...], w_ref[...], preferred_element_type=F32)
    c = zu_ref.shape[-1]
    zu_ref[...] = _gelu(z[:, :c]).astype(zu_ref.dtype)
    zv_ref[...] = _layer_norm(_gelu(z[:, c:]), g_ref[...], b_ref[...]).astype(zv_ref.dtype)


def _in_projection(h, w_in, e, sg, sb):
    n, d = h.shape
    c = w_in.shape[-1] // 4
    h_spec = pl.BlockSpec((TM, d), lambda i: (i, 0))
    vec_spec = pl.BlockSpec((1, c), lambda i: (0, 0))
    out_spec = pl.BlockSpec((TM, c), lambda i: (i, 0))
    a = pl.pallas_call(
        _glu_in_kernel,
        grid=(n // TM,),
        in_specs=[h_spec, pl.BlockSpec((None, d, 2 * c), lambda i: (e, 0, 0))],
        out_specs=out_spec,
        out_shape=jax.ShapeDtypeStruct((n, c), F32),
        compiler_params=_params(1),
        name="glu_in",
    )(h, w_in)
    zu, zv = pl.pallas_call(
        _sgu_in_kernel,
        grid=(n // TM,),
        in_specs=[h_spec, pl.BlockSpec((None, d, 2 * c), lambda i: (e, 0, 1)), vec_spec, vec_spec],
        out_specs=[out_spec, out_spec],
        out_shape=[jax.ShapeDtypeStruct((n, c), BF16)] * 2,
        compiler_params=_params(1),
        name="sgu_in",
    )(h, w_in, sg, sb)
    return a, zu, zv


def _even_mix_kernel(ap_ref, a_ref, an_ref, zu_ref, zv_ref, x_ref, dw_ref, cg_ref, cb_ref,
                     wsp_ref, bsp_ref, wo_ref, g_post_ref, g_next_ref,
                     xo_ref, ho_ref, win_ref, conv_ref, cat_ref):
    i = pl.program_id(1)
    ts, c = a_ref.shape
    win_ref[0:HALO, :] = jnp.where(i > 0, ap_ref[...], 0.0)
    win_ref[HALO:HALO + ts, :] = a_ref[...]
    win_ref[HALO + ts:, :] = jnp.where(i < pl.num_programs(1) - 1, an_ref[...], 0.0)
    first = HALO - CONV_WIDTH // 2
    for r0 in range(0, ts, CONV_ROWS):
        acc = jnp.zeros((CONV_ROWS, c), F32)
        for w in range(CONV_WIDTH):
            acc = acc + win_ref[r0 + first + w:r0 + first + w + CONV_ROWS, :] * dw_ref[w:w + 1, :]
        conv_ref[r0:r0 + CONV_ROWS, :] = acc
    cat_ref[:, :c] = _silu(_layer_norm(conv_ref[...], cg_ref[...], cb_ref[...])).astype(cat_ref.dtype)
    gd = c // SGU_GROUPS
    for r0 in range(0, ts, CHUNK):
        for g in range(SGU_GROUPS):
            cols = slice(g * gd, (g + 1) * gd)
            s = jnp.dot(wsp_ref[g], zv_ref[r0:r0 + CHUNK, cols], preferred_element_type=F32)
            s = s + bsp_ref[:, cols]
            gated = zu_ref[r0:r0 + CHUNK, cols].astype(F32) * s
            cat_ref[r0:r0 + CHUNK, c + g * gd:c + (g + 1) * gd] = gated.astype(cat_ref.dtype)
    m = jnp.dot(cat_ref[...], wo_ref[...], preferred_element_type=F32)
    _residual_epilogue(m, x_ref, g_post_ref, g_next_ref, xo_ref, ho_ref)


def _even_mix(a, zu, zv, x, dw, cg, cb, wsp, bsp, w_out, e, g_post, g_next):
    b, s, c = a.shape
    d = x.shape[-1]
    nblk = TS // HALO
    tile = lambda width: pl.BlockSpec((None, TS, width), lambda bi, i: (bi, i, 0))
    const2 = lambda shape: pl.BlockSpec(shape, lambda bi, i: (0, 0))
    return pl.pallas_call(
        _even_mix_kernel,
        grid=(b, s // TS),
        in_specs=[
            pl.BlockSpec((None, HALO, c), lambda bi, i: (bi, jnp.maximum(i * nblk - 1, 0), 0)),
            tile(c),
            pl.BlockSpec((None, HALO, c),
                         lambda bi, i: (bi, jnp.minimum((i + 1) * nblk, s // HALO - 1), 0)),
            tile(c), tile(c), tile(d),
            const2((CONV_WIDTH, c)), const2((1, c)), const2((1, c)),
            pl.BlockSpec((SGU_GROUPS, CHUNK, CHUNK), lambda bi, i: (0, 0, 0)),
            const2((CHUNK, c)),
            pl.BlockSpec((None, 2 * c, d), lambda bi, i: (e, 0, 0)),
            const2((1, d)), const2((1, d)),
        ],
        out_specs=[tile(d), tile(d)],
        out_shape=[jax.ShapeDtypeStruct((b, s, d), F32), jax.ShapeDtypeStruct((b, s, d), BF16)],
        scratch_shapes=[pltpu.VMEM((TS + 2 * HALO, c), F32), pltpu.VMEM((TS, c), F32),
                        pltpu.VMEM((TS, 2 * c), BF16)],
        compiler_params=_params(2),
        name="even_mix",
    )(a, a, a, zu, zv, x, dw, cg, cb, wsp, bsp, w_out, g_post, g_next)


def _qkv_kernel(h_ref, w_ref, o_ref, *, q_scale):
    z = jnp.dot(h_ref[...], w_ref[...], preferred_element_type=F32)
    scale = jnp.where(pl.program_id(0) == 0, q_scale, 1.0)
    o_ref[...] = (z * scale).astype(o_ref.dtype)


def _qkv_projection(h, w_qkv, o):
    n, d = h.shape
    width = w_qkv.shape[-1] // 3
    return pl.pallas_call(
        functools.partial(_qkv_kernel, q_scale=C_HEAD_DIM ** -0.5),
        grid=(3, n // TM),
        in_specs=[pl.BlockSpec((TM, d), lambda j, i: (i, 0)),
                  pl.BlockSpec((None, d, width), lambda j, i: (o, 0, j))],
        out_specs=pl.BlockSpec((TM, width), lambda j, i: (i, j)),
        out_shape=jax.ShapeDtypeStruct((n, 3 * width), BF16),
        compiler_params=_params(2),
        name="qkv_proj",
    )(h, w_qkv)


def _t5_bucket(rel):
    nb = NUM_BUCKETS // 2
    max_exact = nb // 2
    ret = jnp.where(rel > 0, nb, 0)
    n = jnp.abs(rel)
    nf = jnp.maximum(n, 1).astype(jnp.float32)
    large = max_exact + (jnp.log(nf / max_exact) / math.log(MAX_DISTANCE / max_exact)
                         * (nb - max_exact)).astype(jnp.int32)
    large = jnp.minimum(large, nb - 1)
    return ret + jnp.where(n < max_exact, n, large)


BAND_TILES = 5


def _bias_band_kernel(bucket_ref, rb_ref, band_ref):
    h = pl.program_id(0)
    t = band_ref.shape[-1]
    width = (BAND_TILES + 1) * t
    bucket = bucket_ref[...]
    row = jnp.zeros(bucket.shape, F32)
    for bkt in range(NUM_BUCKETS):
        row = jnp.where(bucket == bkt, rb_ref[bkt, h], row)
    rolled = pltpu.roll(jnp.broadcast_to(row, (t, width)), 0, 1, stride=1, stride_axis=0)
    for d in range(BAND_TILES):
        band_ref[d] = rolled[:, (d + 1) * t:(d + 2) * t]


def _bias_band(rel_bias):
    t = TQ
    assert t >= MAX_DISTANCE
    half = (BAND_TILES + 1) * t // 2
    bucket = _t5_bucket(jnp.arange(-half, half, dtype=jnp.int32))[None, :]
    return pl.pallas_call(
        _bias_band_kernel,
        grid=(C_HEADS,),
        in_specs=[pl.BlockSpec((1, 2 * half), lambda h: (0, 0)),
                  pl.BlockSpec(memory_space=pltpu.SMEM)],
        out_specs=pl.BlockSpec((None, BAND_TILES, t, t), lambda h: (h, 0, 0, 0)),
        out_shape=jax.ShapeDtypeStruct((C_HEADS, BAND_TILES, t, t), F32),
        compiler_params=_params(1),
        name="bias_band",
    )(bucket, rel_bias)


def _diff_attn_kernel(q_ref, k_ref, v_ref, band_ref, lam_ref, sg_ref, o_ref, s_ref,
                      *, lambda_init):
    qi = pl.program_id(2)
    t, dh2 = q_ref.shape
    dh = dh2 // 2
    nk = k_ref.shape[0] // t
    half = t // 2
    maps = range(2)
    mx = [None, None]
    for kj in range(nk):
        bias = band_ref[jnp.clip(kj - qi, -2, 2) + 2]
        for c in maps:
            q = q_ref[:, c * dh:(c + 1) * dh]
            k = k_ref[kj * t:(kj + 1) * t, c * dh:(c + 1) * dh]
            s = lax.dot_general(q, k, (((1,), (1,)), ((), ())), preferred_element_type=F32)
            s = s + bias
            s_ref[c, kj] = s
            part = jnp.maximum(s[:, :half], s[:, half:])
            mx[c] = part if mx[c] is None else jnp.maximum(mx[c], part)
    m = [jnp.broadcast_to(jnp.max(mx[c], axis=1, keepdims=True), (t, half)) for c in maps]
    ls = [None, None]
    acc = [None, None]
    for kj in range(nk):
        v = v_ref[kj * t:(kj + 1) * t, :]
        for c in maps:
            e_lo = jnp.exp(s_ref[c, kj, :, :half] - m[c])
            e_hi = jnp.exp(s_ref[c, kj, :, half:] - m[c])
            ls[c] = e_lo + e_hi if ls[c] is None else ls[c] + (e_lo + e_hi)
            p = jnp.concatenate([e_lo.astype(v.dtype), e_hi.astype(v.dtype)], axis=1)
            pv = jnp.dot(p, v, preferred_element_type=F32)
            acc[c] = pv if acc[c] is None else acc[c] + pv
    outs = [acc[c] / jnp.sum(ls[c], axis=1, keepdims=True) for c in maps]
    lq = lam_ref[...]
    lam = (jnp.exp(jnp.sum(lq[0:1] * lq[1:2], axis=1, keepdims=True))
           - jnp.exp(jnp.sum(lq[2:3] * lq[3:4], axis=1, keepdims=True)) + lambda_init)
    o = outs[0] - lam * outs[1]
    o_ref[...] = (_rms(o, sg_ref[...]) * (1.0 - lambda_init)).astype(o_ref.dtype)


def _diff_attention(qkv, band, lam_qk, subln_g, lambda_init):
    b, s, _ = qkv.shape
    dh2 = 2 * C_HEAD_DIM
    t = TQ
    return pl.pallas_call(
        functools.partial(_diff_attn_kernel, lambda_init=lambda_init),
        grid=(b, C_HEADS, s // t),
        in_specs=[
            pl.BlockSpec((None, t, dh2), lambda bi, h, qi: (bi, qi, h)),
            pl.BlockSpec((None, s, dh2), lambda bi, h, qi: (bi, 0, C_HEADS + h)),
            pl.BlockSpec((None, s, dh2), lambda bi, h, qi: (bi, 0, 2 * C_HEADS + h)),
            pl.BlockSpec((None, BAND_TILES, t, t), lambda bi, h, qi: (h, 0, 0, 0)),
            pl.BlockSpec((4, C_HEAD_DIM), lambda bi, h, qi: (0, 0)),
            pl.BlockSpec((1, dh2), lambda bi, h, qi: (0, 0)),
        ],
        out_specs=pl.BlockSpec((None, t, dh2), lambda bi, h, qi: (bi, qi, h)),
        out_shape=jax.ShapeDtypeStruct((b, s, C_HEADS * dh2), BF16),
        scratch_shapes=[pltpu.VMEM((2, s // t, t, t), F32)],
        compiler_params=_params(3),
        name="diff_attn",
    )(qkv, qkv, qkv, band, lam_qk, subln_g)


def _out_proj_kernel(a_ref, w_ref, x_ref, g_post_ref, g_next_ref, xo_ref, ho_ref):
    m = jnp.dot(a_ref[...], w_ref[...], preferred_element_type=F32)
    _residual_epilogue(m, x_ref, g_post_ref, g_next_ref, xo_ref, ho_ref)


def _out_projection(a, w, o, x, g_post, g_next):
    n, k = a.shape
    d = x.shape[-1]
    row = lambda width: pl.BlockSpec((TM, width), lambda i: (i, 0))
    vec = pl.BlockSpec((1, d), lambda i: (0, 0))
    return pl.pallas_call(
        _out_proj_kernel,
        grid=(n // TM,),
        in_specs=[row(k), pl.BlockSpec((None, k, d), lambda i: (o, 0, 0)), row(d), vec, vec],
        out_specs=[row(d), row(d)],
        out_shape=[jax.ShapeDtypeStruct((n, d), F32), jax.ShapeDtypeStruct((n, d), BF16)],
        compiler_params=_params(1),
        name="out_proj",
    )(a, w, x, g_post, g_next)


def _ffn_kernel(h_ref, wg_ref, wu_ref, wd_ref, x_ref, g_post_ref, g_next_ref,
                xo_ref, ho_ref, acc_ref):
    f = pl.program_id(1)
    h = h_ref[...]
    gate = jnp.dot(h, wg_ref[...], preferred_element_type=F32)
    up = jnp.dot(h, wu_ref[...], preferred_element_type=F32)
    act = (_silu(gate) * up).astype(h.dtype)
    part = jnp.dot(act, wd_ref[...], preferred_element_type=F32)

    @pl.when(f == 0)
    def _():
        acc_ref[...] = part

    @pl.when(f > 0)
    def _():
        acc_ref[...] += part

    @pl.when(f == pl.num_programs(1) - 1)
    def _():
        _residual_epilogue(acc_ref[...], x_ref, g_post_ref, g_next_ref, xo_ref, ho_ref)


def _ffn(h, wg, wu, wd, layer, x, g_post, g_next):
    n, d = h.shape
    ff = wg.shape[-1]
    row = lambda width: pl.BlockSpec((TM, width), lambda i, f: (i, 0))
    vec = pl.BlockSpec((1, d), lambda i, f: (0, 0))
    return pl.pallas_call(
        _ffn_kernel,
        grid=(n // TM, ff // TF),
        in_specs=[row(d),
                  pl.BlockSpec((None, d, TF), lambda i, f: (layer, 0, f)),
                  pl.BlockSpec((None, d, TF), lambda i, f: (layer, 0, f)),
                  pl.BlockSpec((None, TF, d), lambda i, f: (layer, f, 0)),
                  row(d), vec, vec],
        out_specs=[row(d), row(d)],
        out_shape=[jax.ShapeDtypeStruct((n, d), F32), jax.ShapeDtypeStruct((n, d), BF16)],
        scratch_shapes=[pltpu.VMEM((TM, d), F32)],
        compiler_params=_params(2),
        name="ffn",
    )(h, wg, wu, wd, x, g_post, g_next)


def kernel(x, norm_g, w_in_ab, dw_kernel, conv_ln_g, conv_ln_b, sgu_ln_g, sgu_ln_b, w_spatial,
           b_spatial, w_out_ab, w_qkv_c, lambda_qk, subln_g, w_out_c, rel_bias, w_gate, w_up,
           w_down):
    b, s, d = x.shape
    n = b * s
    depth = norm_g.shape[0]
    c = conv_ln_g.shape[-1]
    gd = c // SGU_GROUPS

    w_in = w_in_ab.astype(BF16)
    w_out_e = w_out_ab.astype(BF16)
    w_qkv = w_qkv_c.astype(BF16)
    w_out_o = w_out_c.astype(BF16)
    wsp = w_spatial.astype(BF16)
    wg, wu, wd = w_gate.astype(BF16), w_up.astype(BF16), w_down.astype(BF16)
    band = _bias_band(rel_bias)

    x2 = x.reshape(n, d)
    h = _prenorm(x2, norm_g[0, 0][None, :])
    for layer in range(depth):
        g_post = norm_g[layer, 1][None, :]
        g_ffn = norm_g[layer, 2][None, :]
        if layer % 2 == 0:
            e = layer // 2
            a, zu, zv = _in_projection(h, w_in, e, sgu_ln_g[e][None, :], sgu_ln_b[e][None, :])
            bsp = jnp.repeat(b_spatial[e].T, gd, axis=1)
            x3, h3 = _even_mix(a.reshape(b, s, c), zu.reshape(b, s, c), zv.reshape(b, s, c),
                               x2.reshape(b, s, d), dw_kernel[e], conv_ln_g[e][None, :],
                               conv_ln_b[e][None, :], wsp[e], bsp, w_out_e, e, g_post, g_ffn)
            x2, h = x3.reshape(n, d), h3.reshape(n, d)
        else:
            o = layer // 2
            lambda_init = 0.8 - 0.6 * math.exp(-0.3 * layer)
            qkv = _qkv_projection(h, w_qkv, o)
            att = _diff_attention(qkv.reshape(b, s, -1), band, lambda_qk[o],
                                  subln_g[o][None, :], lambda_init)
            x2, h = _out_projection(att.reshape(n, -1), w_out_o, o, x2, g_post, g_ffn)
        g_next = norm_g[(layer + 1) % depth, 0][None, :]
        x2, h = _ffn(h, wg, wu, wd, layer, x2, norm_g[layer, 3][None, :], g_next)
    return x2.reshape(b, s, d)
```

```python
import functools
import math

import jax
import jax.numpy as jnp
from jax import lax
from jax.experimental import pallas as pl
from jax.experimental.pallas import tpu as pltpu

F32 = jnp.float32
BF16 = jnp.bfloat16

EPS = 1e-6
CONV_WIDTH = 31
CHUNK = 128
SGU_GROUPS = 8
C_HEADS = 8
C_HEAD_DIM = 128
NUM_BUCKETS = 32
MAX_DISTANCE = 128
LOG2E = math.log2(math.e)

VMEM_LIMIT_BYTES = 56 * 1024 * 1024
SUBLANES = 8
HALO = 16

TM = 512
TF = 512
TS = 256
TQ = 256
CONV_ROWS = 32
BAND_TILES = 5


def _params(grid_rank):
    return pltpu.CompilerParams(dimension_semantics=("arbitrary",) * grid_rank,
                                vmem_limit_bytes=VMEM_LIMIT_BYTES)


def _rms(v, g):
    return (v * lax.rsqrt(jnp.mean(v * v, axis=-1, keepdims=True) + EPS)) * g


def _layer_norm(v, g, b):
    mu = jnp.mean(v, axis=-1, keepdims=True)
    vc = v - mu
    return (vc * lax.rsqrt(jnp.mean(vc * vc, axis=-1, keepdims=True) + EPS)) * g + b


def _gelu(v):
    return 0.5 * v * (1.0 + lax.erf(v * math.sqrt(0.5)))


def _silu(v):
    return v * jax.nn.sigmoid(v)


def _residual_epilogue(m, x_ref, g_post_ref, g_next_ref, xo_ref, ho_ref):
    xn = x_ref[...] + _rms(m, g_post_ref[...])
    xo_ref[...] = xn
    if ho_ref is not None:
        ho_ref[...] = _rms(xn, g_next_ref[...]).astype(ho_ref.dtype)


def _prenorm_kernel(x_ref, g_ref, h_ref):
    h_ref[...] = _rms(x_ref[...], g_ref[...]).astype(h_ref.dtype)


def _prenorm(x2, g):
    n, d = x2.shape
    return pl.pallas_call(
        _prenorm_kernel,
        grid=(n // TM,),
        in_specs=[pl.BlockSpec((TM, d), lambda i: (i, 0)),
                  pl.BlockSpec((1, d), lambda i: (0, 0))],
        out_specs=pl.BlockSpec((TM, d), lambda i: (i, 0)),
        out_shape=jax.ShapeDtypeStruct((n, d), BF16),
        compiler_params=_params(1),
        name="prenorm",
    )(x2, g)


def _glu_in_kernel(h_ref, w_ref, a_ref):
    z = jnp.dot(h_ref[...], w_ref[...], preferred_element_type=F32)
    c = a_ref.shape[-1]
    a_ref[...] = z[:, :c] * jax.nn.sigmoid(z[:, c:])


def _sgu_in_kernel(h_ref, w_ref, g_ref, b_ref, zu_ref, zv_ref):
    z = jnp.dot(h_ref[...], w_ref[...], preferred_element_type=F32)
    c = zu_ref.shape[-1]
    zu_ref[...] = _gelu(z[:, :c]).astype(zu_ref.dtype)
    zv_ref[...] = _layer_norm(_gelu(z[:, c:]), g_ref[...], b_ref[...]).astype(zv_ref.dtype)


def _in_projection(h, w_in, e, sg, sb):
    n, d = h.shape
    c = w_in.shape[-1] // 4
    h_spec = pl.BlockSpec((TM, d), lambda i: (i, 0))
    vec_spec = pl.BlockSpec((1, c), lambda i: (0, 0))
    out_spec = pl.BlockSpec((TM, c), lambda i: (i, 0))
    a = pl.pallas_call(
        _glu_in_kernel,
        grid=(n // TM,),
        in_specs=[h_spec, pl.BlockSpec((None, d, 2 * c), lambda i: (e, 0, 0))],
        out_specs=out_spec,
        out_shape=jax.ShapeDtypeStruct((n, c), F32),
        compiler_params=_params(1),
        name="glu_in",
    )(h, w_in)
    zu, zv = pl.pallas_call(
        _sgu_in_kernel,
        grid=(n // TM,),
        in_specs=[h_spec, pl.BlockSpec((None, d, 2 * c), lambda i: (e, 0, 1)), vec_spec, vec_spec],
        out_specs=[out_spec, out_spec],
        out_shape=[jax.ShapeDtypeStruct((n, c), BF16)] * 2,
        compiler_params=_params(1),
        name="sgu_in",
    )(h, w_in, sg, sb)
    return a, zu, zv


def _even_mix_kernel(ap_ref, a_ref, an_ref, zu_ref, zv_ref, x_ref, dw_ref, cg_ref, cb_ref,
                     wsp_ref, bsp_ref, wo_ref, g_post_ref, g_next_ref,
                     xo_ref, ho_ref, win_ref, shift_ref, conv_ref, cat_ref):
    i = pl.program_id(1)
    ts, c = a_ref.shape
    win_ref[0:HALO, :] = jnp.where(i > 0, ap_ref[...], 0.0)
    win_ref[HALO:HALO + ts, :] = a_ref[...]
    win_ref[HALO + ts:, :] = jnp.where(i < pl.num_programs(1) - 1, an_ref[...], 0.0)
    span = shift_ref.shape[1]
    for r in range(1, SUBLANES):
        shift_ref[r - 1] = win_ref[r:r + span, :]
    first = HALO - CONV_WIDTH // 2
    for r0 in range(0, ts, CONV_ROWS):
        acc = jnp.zeros((CONV_ROWS // SUBLANES, SUBLANES, c), F32)
        for w in range(CONV_WIDTH):
            q, r = divmod(first + w, SUBLANES)
            rows = slice(r0 + q * SUBLANES, r0 + q * SUBLANES + CONV_ROWS)
            tap = win_ref[rows, :] if r == 0 else shift_ref[r - 1, rows, :]
            taps = dw_ref[w * SUBLANES:(w + 1) * SUBLANES, :]
            acc = acc + tap.reshape(-1, SUBLANES, c) * taps[None]
        conv_ref[r0:r0 + CONV_ROWS, :] = acc.reshape(CONV_ROWS, c)
    cat_ref[:, :c] = _silu(_layer_norm(conv_ref[...], cg_ref[...], cb_ref[...])).astype(cat_ref.dtype)
    gd = c // SGU_GROUPS
    for r0 in range(0, ts, CHUNK):
        for g in range(SGU_GROUPS):
            cols = slice(g * gd, (g + 1) * gd)
            s = jnp.dot(wsp_ref[g], zv_ref[r0:r0 + CHUNK, cols], preferred_element_type=F32)
            s = s + bsp_ref[:, cols]
            gated = zu_ref[r0:r0 + CHUNK, cols].astype(F32) * s
            cat_ref[r0:r0 + CHUNK, c + g * gd:c + (g + 1) * gd] = gated.astype(cat_ref.dtype)
    m = jnp.dot(cat_ref[...], wo_ref[...], preferred_element_type=F32)
    _residual_epilogue(m, x_ref, g_post_ref, g_next_ref, xo_ref, ho_ref)


def _even_mix(a, zu, zv, x, dw, cg, cb, wsp, bsp, w_out, e, g_post, g_next):
    b, s, c = a.shape
    d = x.shape[-1]
    nblk = TS // HALO
    tile = lambda width: pl.BlockSpec((None, TS, width), lambda bi, i: (bi, i, 0))
    const2 = lambda shape: pl.BlockSpec(shape, lambda bi, i: (0, 0))
    return pl.pallas_call(
        _even_mix_kernel,
        grid=(b, s // TS),
        in_specs=[
            pl.BlockSpec((None, HALO, c), lambda bi, i: (bi, jnp.maximum(i * nblk - 1, 0), 0)),
            tile(c),
            pl.BlockSpec((None, HALO, c),
                         lambda bi, i: (bi, jnp.minimum((i + 1) * nblk, s // HALO - 1), 0)),
            tile(c), tile(c), tile(d),
            const2((CONV_WIDTH * SUBLANES, c)), const2((1, c)), const2((1, c)),
            pl.BlockSpec((SGU_GROUPS, CHUNK, CHUNK), lambda bi, i: (0, 0, 0)),
            const2((CHUNK, c)),
            pl.BlockSpec((None, 2 * c, d), lambda bi, i: (e, 0, 0)),
            const2((1, d)), const2((1, d)),
        ],
        out_specs=[tile(d), tile(d)],
        out_shape=[jax.ShapeDtypeStruct((b, s, d), F32), jax.ShapeDtypeStruct((b, s, d), BF16)],
        scratch_shapes=[pltpu.VMEM((TS + 2 * HALO, c), F32),
                        pltpu.VMEM((SUBLANES - 1, TS + 2 * HALO - SUBLANES, c), F32),
                        pltpu.VMEM((TS, c), F32), pltpu.VMEM((TS, 2 * c), BF16)],
        compiler_params=_params(2),
        name="even_mix",
    )(a, a, a, zu, zv, x, dw, cg, cb, wsp, bsp, w_out, g_post, g_next)


def _qkv_kernel(h_ref, w_ref, o_ref, *, q_scale):
    z = jnp.dot(h_ref[...], w_ref[...], preferred_element_type=F32)
    scale = jnp.where(pl.program_id(0) == 0, q_scale, 1.0)
    o_ref[...] = (z * scale).astype(o_ref.dtype)


def _qkv_projection(h, w_qkv, o):
    n, d = h.shape
    width = w_qkv.shape[-1] // 3
    return pl.pallas_call(
        functools.partial(_qkv_kernel, q_scale=C_HEAD_DIM ** -0.5 * LOG2E),
        grid=(3, n // TM),
        in_specs=[pl.BlockSpec((TM, d), lambda j, i: (i, 0)),
                  pl.BlockSpec((None, d, width), lambda j, i: (o, 0, j))],
        out_specs=pl.BlockSpec((TM, width), lambda j, i: (i, j)),
        out_shape=jax.ShapeDtypeStruct((n, 3 * width), BF16),
        compiler_params=_params(2),
        name="qkv_proj",
    )(h, w_qkv)


def _t5_bucket(rel):
    nb = NUM_BUCKETS // 2
    max_exact = nb // 2
    ret = jnp.where(rel > 0, nb, 0)
    n = jnp.abs(rel)
    nf = jnp.maximum(n, 1).astype(jnp.float32)
    large = max_exact + (jnp.log(nf / max_exact) / math.log(MAX_DISTANCE / max_exact)
                         * (nb - max_exact)).astype(jnp.int32)
    large = jnp.minimum(large, nb - 1)
    return ret + jnp.where(n < max_exact, n, large)


def _bias_band_kernel(bucket_ref, rb_ref, band_ref):
    h = pl.program_id(0)
    t = band_ref.shape[-1]
    width = (BAND_TILES + 1) * t
    bucket = bucket_ref[...]
    row = jnp.zeros(bucket.shape, F32)
    for bkt in range(NUM_BUCKETS):
        row = jnp.where(bucket == bkt, rb_ref[bkt, h] * LOG2E, row)
    rolled = pltpu.roll(jnp.broadcast_to(row, (t, width)), 0, 1, stride=1, stride_axis=0)
    for d in range(BAND_TILES):
        band_ref[d] = rolled[:, (d + 1) * t:(d + 2) * t]


def _bias_band(rel_bias):
    t = TQ
    assert t >= MAX_DISTANCE
    half = (BAND_TILES + 1) * t // 2
    bucket = _t5_bucket(jnp.arange(-half, half, dtype=jnp.int32))[None, :]
    return pl.pallas_call(
        _bias_band_kernel,
        grid=(C_HEADS,),
        in_specs=[pl.BlockSpec((1, 2 * half), lambda h: (0, 0)),
                  pl.BlockSpec(memory_space=pltpu.SMEM)],
        out_specs=pl.BlockSpec((None, BAND_TILES, t, t), lambda h: (h, 0, 0, 0)),
        out_shape=jax.ShapeDtypeStruct((C_HEADS, BAND_TILES, t, t), F32),
        compiler_params=_params(1),
        name="bias_band",
    )(bucket, rel_bias)


def _attn_step(qi, q_ref, k_ref, v_ref, band_ref, lam_ref, sg_ref, o_ref,
               s_new, m_new, s_old, m_old, lambda_init):
    t, dh2 = q_ref.shape
    dh = dh2 // 2
    half = t // 2
    maps = range(2)
    mx = [None, None]
    ls = [None, None]
    acc = [None, None]
    for kj in range(k_ref.shape[0] // t):
        v = v_ref[kj * t:(kj + 1) * t, :]
        for c in maps:
            e_lo = jnp.exp2(s_old[c, kj, :, :half] - m_old[c])
            e_hi = jnp.exp2(s_old[c, kj, :, half:] - m_old[c])
            ls[c] = e_lo + e_hi if ls[c] is None else ls[c] + (e_lo + e_hi)
            p = jnp.concatenate([e_lo.astype(v.dtype), e_hi.astype(v.dtype)], axis=1)
            pv = jnp.dot(p, v, preferred_element_type=F32)
            acc[c] = pv if acc[c] is None else acc[c] + pv
        bias = band_ref[jnp.clip(kj - qi, -2, 2) + 2]
        for c in maps:
            q = q_ref[:, c * dh:(c + 1) * dh]
            k = k_ref[kj * t:(kj + 1) * t, c * dh:(c + 1) * dh]
            s = lax.dot_general(q, k, (((1,), (1,)), ((), ())), preferred_element_type=F32)
            s = s + bias
            s_new[c, kj] = s
            part = jnp.maximum(s[:, :half], s[:, half:])
            mx[c] = part if mx[c] is None else jnp.maximum(mx[c], part)
    for c in maps:
        m_new[c] = jnp.broadcast_to(jnp.max(mx[c], axis=1, keepdims=True), (t, half))
    outs = [acc[c] / jnp.sum(ls[c], axis=1, keepdims=True) for c in maps]
    lq = lam_ref[...]
    lam = (jnp.exp(jnp.sum(lq[0:1] * lq[1:2], axis=1, keepdims=True))
           - jnp.exp(jnp.sum(lq[2:3] * lq[3:4], axis=1, keepdims=True)) + lambda_init)
    o = outs[0] - lam * outs[1]
    o_ref[...] = (_rms(o, sg_ref[...]) * (1.0 - lambda_init)).astype(o_ref.dtype)


def _diff_attn_kernel(q_ref, k_ref, v_ref, band_ref, lam_ref, sg_ref, o_ref,
                      s_even, s_odd, m_even, m_odd, *, lambda_init, tiles_per_head):
    i = pl.program_id(0)
    tile = jnp.minimum(i, pl.num_programs(0) - 2)
    qi = tile % tiles_per_head
    step = functools.partial(_attn_step, qi, q_ref, k_ref, v_ref, band_ref, lam_ref, sg_ref, o_ref)

    @pl.when(i == 0)
    def _():
        s_odd[...] = jnp.zeros(s_odd.shape, F32)
        m_odd[...] = jnp.zeros(m_odd.shape, F32)

    @pl.when(i % 2 == 0)
    def _():
        step(s_even, m_even, s_odd, m_odd, lambda_init)

    @pl.when(i % 2 == 1)
    def _():
        step(s_odd, m_odd, s_even, m_even, lambda_init)


def _diff_attention(qkv, band, lam_qk, subln_g, lambda_init):
    b, s, _ = qkv.shape
    dh2 = 2 * C_HEAD_DIM
    t = TQ
    nq = s // t
    tiles = b * C_HEADS * nq

    def scored(i):
        tile = jnp.minimum(i, tiles - 1)
        return tile // (C_HEADS * nq), (tile // nq) % C_HEADS, tile % nq

    def finished(i):
        tile = jnp.maximum(i - 1, 0)
        return tile // (C_HEADS * nq), (tile // nq) % C_HEADS, tile % nq

    def q_map(i):
        bi, h, qi = scored(i)
        return bi, qi, h

    def k_map(i):
        bi, h, _ = scored(i)
        return bi, 0, C_HEADS + h

    def v_map(i):
        bi, h, _ = finished(i)
        return bi, 0, 2 * C_HEADS + h

    def o_map(i):
        bi, h, qi = finished(i)
        return bi, qi, h

    scores = pltpu.VMEM((2, nq, t, t), F32)
    maxima = pltpu.VMEM((2, t, t // 2), F32)
    return pl.pallas_call(
        functools.partial(_diff_attn_kernel, lambda_init=lambda_init, tiles_per_head=nq),
        grid=(tiles + 1,),
        in_specs=[
            pl.BlockSpec((None, t, dh2), q_map),
            pl.BlockSpec((None, s, dh2), k_map),
            pl.BlockSpec((None, s, dh2), v_map),
            pl.BlockSpec((None, BAND_TILES, t, t), lambda i: (scored(i)[1], 0, 0, 0)),
            pl.BlockSpec((4, C_HEAD_DIM), lambda i: (0, 0)),
            pl.BlockSpec((1, dh2), lambda i: (0, 0)),
        ],
        out_specs=pl.BlockSpec((None, t, dh2), o_map),
        out_shape=jax.ShapeDtypeStruct((b, s, C_HEADS * dh2), BF16),
        scratch_shapes=[scores, scores, maxima, maxima],
        compiler_params=_params(1),
        name="diff_attn",
    )(qkv, qkv, qkv, band, lam_qk, subln_g)


def _out_proj_kernel(a_ref, w_ref, x_ref, g_post_ref, g_next_ref, xo_ref, ho_ref):
    m = jnp.dot(a_ref[...], w_ref[...], preferred_element_type=F32)
    _residual_epilogue(m, x_ref, g_post_ref, g_next_ref, xo_ref, ho_ref)


def _out_projection(a, w, o, x, g_post, g_next):
    n, k = a.shape
    d = x.shape[-1]
    row = lambda width: pl.BlockSpec((TM, width), lambda i: (i, 0))
    vec = pl.BlockSpec((1, d), lambda i: (0, 0))
    return pl.pallas_call(
        _out_proj_kernel,
        grid=(n // TM,),
        in_specs=[row(k), pl.BlockSpec((None, k, d), lambda i: (o, 0, 0)), row(d), vec, vec],
        out_specs=[row(d), row(d)],
        out_shape=[jax.ShapeDtypeStruct((n, d), F32), jax.ShapeDtypeStruct((n, d), BF16)],
        compiler_params=_params(1),
        name="out_proj",
    )(a, w, x, g_post, g_next)


def _ffn_kernel(h_ref, wg_ref, wu_ref, wd_ref, x_ref, g_post_ref, g_next_ref, xo_ref, *maybe_ho_ref):
    f = pl.program_id(1)

    @pl.when(f == 0)
    def _():
        xo_ref[...] = jnp.zeros(xo_ref.shape, F32)

    h = h_ref[...]
    gate = jnp.dot(h, wg_ref[...], preferred_element_type=F32)
    up = jnp.dot(h, wu_ref[...], preferred_element_type=F32)
    act = (_silu(gate) * up).astype(h.dtype)
    xo_ref[...] += jnp.dot(act, wd_ref[...], preferred_element_type=F32)

    @pl.when(f == pl.num_programs(1) - 1)
    def _():
        ho_ref = maybe_ho_ref[0] if maybe_ho_ref else None
        _residual_epilogue(xo_ref[...], x_ref, g_post_ref, g_next_ref, xo_ref, ho_ref)


def _ffn(h, wg, wu, wd, layer, x, g_post, g_next):
    n, d = h.shape
    ff = wg.shape[-1]
    row = lambda width: pl.BlockSpec((TM, width), lambda i, f: (i, 0))
    vec = pl.BlockSpec((1, d), lambda i, f: (0, 0))
    out_specs = [row(d)]
    out_shape = [jax.ShapeDtypeStruct((n, d), F32)]
    if g_next is not None:
        out_specs.append(row(d))
        out_shape.append(jax.ShapeDtypeStruct((n, d), BF16))
    outs = pl.pallas_call(
        _ffn_kernel,
        grid=(n // TM, ff // TF),
        in_specs=[row(d),
                  pl.BlockSpec((None, d, TF), lambda i, f: (layer, 0, f)),
                  pl.BlockSpec((None, d, TF), lambda i, f: (layer, 0, f)),
                  pl.BlockSpec((None, TF, d), lambda i, f: (layer, f, 0)),
                  row(d), vec, vec],
        out_specs=out_specs,
        out_shape=out_shape,
        compiler_params=_params(2),
        name="ffn",
    )(h, wg, wu, wd, x, g_post, g_post if g_next is None else g_next)
    return outs if g_next is not None else (outs[0], None)


def kernel(x, norm_g, w_in_ab, dw_kernel, conv_ln_g, conv_ln_b, sgu_ln_g, sgu_ln_b, w_spatial,
           b_spatial, w_out_ab, w_qkv_c, lambda_qk, subln_g, w_out_c, rel_bias, w_gate, w_up,
           w_down):
    b, s, d = x.shape
    n = b * s
    depth = norm_g.shape[0]
    c = conv_ln_g.shape[-1]
    gd = c // SGU_GROUPS

    w_in = w_in_ab.astype(BF16)
    w_out_e = w_out_ab.astype(BF16)
    w_qkv = w_qkv_c.astype(BF16)
    w_out_o = w_out_c.astype(BF16)
    wsp = w_spatial.astype(BF16)
    wg, wu, wd = w_gate.astype(BF16), w_up.astype(BF16), w_down.astype(BF16)
    band = _bias_band(rel_bias)

    x2 = x.reshape(n, d)
    h = _prenorm(x2, norm_g[0, 0][None, :])
    for layer in range(depth):
        g_post = norm_g[layer, 1][None, :]
        g_ffn = norm_g[layer, 2][None, :]
        if layer % 2 == 0:
            e = layer // 2
            a, zu, zv = _in_projection(h, w_in, e, sgu_ln_g[e][None, :], sgu_ln_b[e][None, :])
            bsp = jnp.repeat(b_spatial[e].T, gd, axis=1)
            taps = jnp.repeat(dw_kernel[e], SUBLANES, axis=0)
            x3, h3 = _even_mix(a.reshape(b, s, c), zu.reshape(b, s, c), zv.reshape(b, s, c),
                               x2.reshape(b, s, d), taps, conv_ln_g[e][None, :],
                               conv_ln_b[e][None, :], wsp[e], bsp, w_out_e, e, g_post, g_ffn)
            x2, h = x3.reshape(n, d), h3.reshape(n, d)
        else:
            o = layer // 2
            lambda_init = 0.8 - 0.6 * math.exp(-0.3 * layer)
            qkv = _qkv_projection(h, w_qkv, o)
            att = _diff_attention(qkv.reshape(b, s, -1), band, lambda_qk[o],
                                  subln_g[o][None, :], lambda_init)
            x2, h = _out_projection(att.reshape(n, -1), w_out_o, o, x2, g_post, g_ffn)
        g_next = norm_g[layer + 1, 0][None, :] if layer + 1 < depth else None
        x2, h = _ffn(h, wg, wu, wd, layer, x2, norm_g[layer, 3][None, :], g_next)
    return x2.reshape(b, s, d)
```

```python
import functools
import math

import jax
import jax.numpy as jnp
from jax import lax
from jax.experimental import pallas as pl
from jax.experimental.pallas import tpu as pltpu

F32 = jnp.float32
BF16 = jnp.bfloat16

EPS = 1e-6
CONV_WIDTH = 31
CHUNK = 128
SGU_GROUPS = 8
C_HEADS = 8
C_HEAD_DIM = 128
NUM_BUCKETS = 32
MAX_DISTANCE = 128
LOG2E = math.log2(math.e)

VMEM_LIMIT_BYTES = 56 * 1024 * 1024
FFN_VMEM_LIMIT_BYTES = 60 * 1024 * 1024
SUBLANES = 8
HALO = 16

TM = 512
TM_FFN = 1024
TF = 512
TS = 256
TQ = 256
CONV_ROWS = 32
FINISH_LEAD = 2
BAND_TILES = 5


def _params(grid_rank, vmem_limit_bytes=VMEM_LIMIT_BYTES):
    return pltpu.CompilerParams(dimension_semantics=("arbitrary",) * grid_rank,
                                vmem_limit_bytes=vmem_limit_bytes)


def _rms(v, g):
    return (v * lax.rsqrt(jnp.mean(v * v, axis=-1, keepdims=True) + EPS)) * g


def _layer_norm(v, g, b):
    mu = jnp.mean(v, axis=-1, keepdims=True)
    vc = v - mu
    return (vc * lax.rsqrt(jnp.mean(vc * vc, axis=-1, keepdims=True) + EPS)) * g + b


def _gelu(v):
    return 0.5 * v * (1.0 + lax.erf(v * math.sqrt(0.5)))


def _silu(v):
    return v * jax.nn.sigmoid(v)


def _residual_epilogue(m, x_ref, g_post_ref, g_next_ref, xo_ref, ho_ref):
    xn = x_ref[...] + _rms(m, g_post_ref[...])
    xo_ref[...] = xn
    if ho_ref is not None:
        ho_ref[...] = _rms(xn, g_next_ref[...]).astype(ho_ref.dtype)


def _prenorm_kernel(x_ref, g_ref, h_ref):
    h_ref[...] = _rms(x_ref[...], g_ref[...]).astype(h_ref.dtype)


def _prenorm(x2, g):
    n, d = x2.shape
    return pl.pallas_call(
        _prenorm_kernel,
        grid=(n // TM,),
        in_specs=[pl.BlockSpec((TM, d), lambda i: (i, 0)),
                  pl.BlockSpec((1, d), lambda i: (0, 0))],
        out_specs=pl.BlockSpec((TM, d), lambda i: (i, 0)),
        out_shape=jax.ShapeDtypeStruct((n, d), BF16),
        compiler_params=_params(1),
        name="prenorm",
    )(x2, g)


def _glu_in_kernel(h_ref, w_ref, a_ref):
    z = jnp.dot(h_ref[...], w_ref[...], preferred_element_type=F32)
    c = a_ref.shape[-1]
    a_ref[...] = z[:, :c] * jax.nn.sigmoid(z[:, c:])


def _sgu_in_kernel(h_ref, w_ref, g_ref, b_ref, zu_ref, zv_ref):
    z = jnp.dot(h_ref[...], w_ref[...], preferred_element_type=F32)
    c = zu_ref.shape[-1]
    zu_ref[...] = _gelu(z[:, :c]).astype(zu_ref.dtype)
    zv_ref[...] = _layer_norm(_gelu(z[:, c:]), g_ref[...], b_ref[...]).astype(zv_ref.dtype)


def _in_projection(h, w_in, e, sg, sb):
    n, d = h.shape
    c = w_in.shape[-1] // 4
    h_spec = pl.BlockSpec((TM, d), lambda i: (i, 0))
    vec_spec = pl.BlockSpec((1, c), lambda i: (0, 0))
    out_spec = pl.BlockSpec((TM, c), lambda i: (i, 0))
    a = pl.pallas_call(
        _glu_in_kernel,
        grid=(n // TM,),
        in_specs=[h_spec, pl.BlockSpec((None, d, 2 * c), lambda i: (e, 0, 0))],
        out_specs=out_spec,
        out_shape=jax.ShapeDtypeStruct((n, c), F32),
        compiler_params=_params(1),
        name="glu_in",
    )(h, w_in)
    zu, zv = pl.pallas_call(
        _sgu_in_kernel,
        grid=(n // TM,),
        in_specs=[h_spec, pl.BlockSpec((None, d, 2 * c), lambda i: (e, 0, 1)), vec_spec, vec_spec],
        out_specs=[out_spec, out_spec],
        out_shape=[jax.ShapeDtypeStruct((n, c), BF16)] * 2,
        compiler_params=_params(1),
        name="sgu_in",
    )(h, w_in, sg, sb)
    return a, zu, zv


def _even_mix_kernel(ap_ref, a_ref, an_ref, zu_ref, zv_ref, x_ref, dw_ref, cg_ref, cb_ref,
                     wsp_ref, bsp_ref, wo_ref, g_post_ref, g_next_ref,
                     xo_ref, ho_ref, win_ref, shift_ref, conv_ref, cat_ref):
    i = pl.program_id(1)
    ts, c = a_ref.shape
    win_ref[0:HALO, :] = jnp.where(i > 0, ap_ref[...], 0.0)
    win_ref[HALO:HALO + ts, :] = a_ref[...]
    win_ref[HALO + ts:, :] = jnp.where(i < pl.num_programs(1) - 1, an_ref[...], 0.0)
    span = shift_ref.shape[1]
    for r in range(1, SUBLANES):
        shift_ref[r - 1] = win_ref[r:r + span, :]
    first = HALO - CONV_WIDTH // 2
    for r0 in range(0, ts, CONV_ROWS):
        acc = jnp.zeros((CONV_ROWS // SUBLANES, SUBLANES, c), F32)
        for w in range(CONV_WIDTH):
            q, r = divmod(first + w, SUBLANES)
            rows = slice(r0 + q * SUBLANES, r0 + q * SUBLANES + CONV_ROWS)
            tap = win_ref[rows, :] if r == 0 else shift_ref[r - 1, rows, :]
            taps = dw_ref[w * SUBLANES:(w + 1) * SUBLANES, :]
            acc = acc + tap.reshape(-1, SUBLANES, c) * taps[None]
        conv_ref[r0:r0 + CONV_ROWS, :] = acc.reshape(CONV_ROWS, c)
    cat_ref[:, :c] = _silu(_layer_norm(conv_ref[...], cg_ref[...], cb_ref[...])).astype(cat_ref.dtype)
    gd = c // SGU_GROUPS
    for r0 in range(0, ts, CHUNK):
        for g in range(SGU_GROUPS):
            cols = slice(g * gd, (g + 1) * gd)
            s = jnp.dot(wsp_ref[g], zv_ref[r0:r0 + CHUNK, cols], preferred_element_type=F32)
            s = s + bsp_ref[:, cols]
            gated = zu_ref[r0:r0 + CHUNK, cols].astype(F32) * s
            cat_ref[r0:r0 + CHUNK, c + g * gd:c + (g + 1) * gd] = gated.astype(cat_ref.dtype)
    m = jnp.dot(cat_ref[...], wo_ref[...], preferred_element_type=F32)
    _residual_epilogue(m, x_ref, g_post_ref, g_next_ref, xo_ref, ho_ref)


def _even_mix(a, zu, zv, x, dw, cg, cb, wsp, bsp, w_out, e, g_post, g_next):
    b, s, c = a.shape
    d = x.shape[-1]
    nblk = TS // HALO
    tile = lambda width: pl.BlockSpec((None, TS, width), lambda bi, i: (bi, i, 0))
    const2 = lambda shape: pl.BlockSpec(shape, lambda bi, i: (0, 0))
    return pl.pallas_call(
        _even_mix_kernel,
        grid=(b, s // TS),
        in_specs=[
            pl.BlockSpec((None, HALO, c), lambda bi, i: (bi, jnp.maximum(i * nblk - 1, 0), 0)),
            tile(c),
            pl.BlockSpec((None, HALO, c),
                         lambda bi, i: (bi, jnp.minimum((i + 1) * nblk, s // HALO - 1), 0)),
            tile(c), tile(c), tile(d),
            const2((CONV_WIDTH * SUBLANES, c)), const2((1, c)), const2((1, c)),
            pl.BlockSpec((SGU_GROUPS, CHUNK, CHUNK), lambda bi, i: (0, 0, 0)),
            const2((CHUNK, c)),
            pl.BlockSpec((None, 2 * c, d), lambda bi, i: (e, 0, 0)),
            const2((1, d)), const2((1, d)),
        ],
        out_specs=[tile(d), tile(d)],
        out_shape=[jax.ShapeDtypeStruct((b, s, d), F32), jax.ShapeDtypeStruct((b, s, d), BF16)],
        scratch_shapes=[pltpu.VMEM((TS + 2 * HALO, c), F32),
                        pltpu.VMEM((SUBLANES - 1, TS + 2 * HALO - SUBLANES, c), F32),
                        pltpu.VMEM((TS, c), F32), pltpu.VMEM((TS, 2 * c), BF16)],
        compiler_params=_params(2),
        name="even_mix",
    )(a, a, a, zu, zv, x, dw, cg, cb, wsp, bsp, w_out, g_post, g_next)


def _qkv_kernel(h_ref, w_ref, o_ref, *, q_scale):
    z = jnp.dot(h_ref[...], w_ref[...], preferred_element_type=F32)
    scale = jnp.where(pl.program_id(0) == 0, q_scale, 1.0)
    o_ref[...] = (z * scale).astype(o_ref.dtype)


def _qkv_projection(h, w_qkv, o):
    n, d = h.shape
    width = w_qkv.shape[-1] // 3
    return pl.pallas_call(
        functools.partial(_qkv_kernel, q_scale=C_HEAD_DIM ** -0.5 * LOG2E),
        grid=(3, n // TM),
        in_specs=[pl.BlockSpec((TM, d), lambda j, i: (i, 0)),
                  pl.BlockSpec((None, d, width), lambda j, i: (o, 0, j))],
        out_specs=pl.BlockSpec((TM, width), lambda j, i: (i, j)),
        out_shape=jax.ShapeDtypeStruct((n, 3 * width), BF16),
        compiler_params=_params(2),
        name="qkv_proj",
    )(h, w_qkv)


def _t5_bucket(rel):
    nb = NUM_BUCKETS // 2
    max_exact = nb // 2
    ret = jnp.where(rel > 0, nb, 0)
    n = jnp.abs(rel)
    nf = jnp.maximum(n, 1).astype(jnp.float32)
    large = max_exact + (jnp.log(nf / max_exact) / math.log(MAX_DISTANCE / max_exact)
                         * (nb - max_exact)).astype(jnp.int32)
    large = jnp.minimum(large, nb - 1)
    return ret + jnp.where(n < max_exact, n, large)


def _bias_band_kernel(bucket_ref, rb_ref, band_ref):
    h = pl.program_id(0)
    t = band_ref.shape[-1]
    width = (BAND_TILES + 1) * t
    bucket = bucket_ref[...]
    row = jnp.zeros(bucket.shape, F32)
    for bkt in range(NUM_BUCKETS):
        row = jnp.where(bucket == bkt, rb_ref[bkt, h] * LOG2E, row)
    rolled = pltpu.roll(jnp.broadcast_to(row, (t, width)), 0, 1, stride=1, stride_axis=0)
    for d in range(BAND_TILES):
        band_ref[d] = rolled[:, (d + 1) * t:(d + 2) * t]


def _bias_band(rel_bias):
    t = TQ
    assert t >= MAX_DISTANCE
    half = (BAND_TILES + 1) * t // 2
    bucket = _t5_bucket(jnp.arange(-half, half, dtype=jnp.int32))[None, :]
    return pl.pallas_call(
        _bias_band_kernel,
        grid=(C_HEADS,),
        in_specs=[pl.BlockSpec((1, 2 * half), lambda h: (0, 0)),
                  pl.BlockSpec(memory_space=pltpu.SMEM)],
        out_specs=pl.BlockSpec((None, BAND_TILES, t, t), lambda h: (h, 0, 0, 0)),
        out_shape=jax.ShapeDtypeStruct((C_HEADS, BAND_TILES, t, t), F32),
        compiler_params=_params(1),
        name="bias_band",
    )(bucket, rel_bias)


def _attn_step(qi, q_ref, k_ref, v_ref, band_ref, lam_ref, sg_ref, o_ref,
               s_new, m_new, s_old, m_old, lambda_init):
    t, dh2 = q_ref.shape
    dh = dh2 // 2
    half = t // 2
    maps = range(2)
    mx = [None, None]
    ls = [None, None]
    acc = [None, None]
    nk = k_ref.shape[0] // t

    def finish(kj):
        v = v_ref[kj * t:(kj + 1) * t, :]
        for c in maps:
            e_lo = jnp.exp2(s_old[c, kj, :, :half] - m_old[c])
            e_hi = jnp.exp2(s_old[c, kj, :, half:] - m_old[c])
            ls[c] = e_lo + e_hi if ls[c] is None else ls[c] + (e_lo + e_hi)
            p = jnp.concatenate([e_lo.astype(v.dtype), e_hi.astype(v.dtype)], axis=1)
            pv = jnp.dot(p, v, preferred_element_type=F32)
            acc[c] = pv if acc[c] is None else acc[c] + pv

    def score(kj):
        bias = band_ref[jnp.clip(kj - qi, -2, 2) + 2]
        for c in maps:
            q = q_ref[:, c * dh:(c + 1) * dh]
            k = k_ref[kj * t:(kj + 1) * t, c * dh:(c + 1) * dh]
            s = lax.dot_general(q, k, (((1,), (1,)), ((), ())), preferred_element_type=F32)
            s = s + bias
            s_new[c, kj] = s
            part = jnp.maximum(s[:, :half], s[:, half:])
            mx[c] = part if mx[c] is None else jnp.maximum(mx[c], part)

    for kj in range(nk + FINISH_LEAD):
        if kj < nk:
            finish(kj)
        if kj >= FINISH_LEAD:
            score(kj - FINISH_LEAD)
    for c in maps:
        m_new[c] = jnp.broadcast_to(jnp.max(mx[c], axis=1, keepdims=True), (t, half))
    outs = [acc[c] / jnp.sum(ls[c], axis=1, keepdims=True) for c in maps]
    lq = lam_ref[...]
    lam = (jnp.exp(jnp.sum(lq[0:1] * lq[1:2], axis=1, keepdims=True))
           - jnp.exp(jnp.sum(lq[2:3] * lq[3:4], axis=1, keepdims=True)) + lambda_init)
    o = outs[0] - lam * outs[1]
    o_ref[...] = (_rms(o, sg_ref[...]) * (1.0 - lambda_init)).astype(o_ref.dtype)


def _diff_attn_kernel(q_ref, k_ref, v_ref, band_ref, lam_ref, sg_ref, o_ref,
                      s_even, s_odd, m_even, m_odd, *, lambda_init, tiles_per_head):
    i = pl.program_id(0)
    tile = jnp.minimum(i, pl.num_programs(0) - 2)
    qi = tile % tiles_per_head
    step = functools.partial(_attn_step, qi, q_ref, k_ref, v_ref, band_ref, lam_ref, sg_ref, o_ref)

    @pl.when(i == 0)
    def _():
        s_odd[...] = jnp.zeros(s_odd.shape, F32)
        m_odd[...] = jnp.zeros(m_odd.shape, F32)

    @pl.when(i % 2 == 0)
    def _():
        step(s_even, m_even, s_odd, m_odd, lambda_init)

    @pl.when(i % 2 == 1)
    def _():
        step(s_odd, m_odd, s_even, m_even, lambda_init)


def _diff_attention(qkv, band, lam_qk, subln_g, lambda_init):
    b, s, _ = qkv.shape
    dh2 = 2 * C_HEAD_DIM
    t = TQ
    nq = s // t
    tiles = b * C_HEADS * nq

    def scored(i):
        tile = jnp.minimum(i, tiles - 1)
        return tile // (C_HEADS * nq), (tile // nq) % C_HEADS, tile % nq

    def finished(i):
        tile = jnp.maximum(i - 1, 0)
        return tile // (C_HEADS * nq), (tile // nq) % C_HEADS, tile % nq

    def q_map(i):
        bi, h, qi = scored(i)
        return bi, qi, h

    def k_map(i):
        bi, h, _ = scored(i)
        return bi, 0, C_HEADS + h

    def v_map(i):
        bi, h, _ = finished(i)
        return bi, 0, 2 * C_HEADS + h

    def o_map(i):
        bi, h, qi = finished(i)
        return bi, qi, h

    scores = pltpu.VMEM((2, nq, t, t), F32)
    maxima = pltpu.VMEM((2, t, t // 2), F32)
    return pl.pallas_call(
        functools.partial(_diff_attn_kernel, lambda_init=lambda_init, tiles_per_head=nq),
        grid=(tiles + 1,),
        in_specs=[
            pl.BlockSpec((None, t, dh2), q_map),
            pl.BlockSpec((None, s, dh2), k_map),
            pl.BlockSpec((None, s, dh2), v_map),
            pl.BlockSpec((None, BAND_TILES, t, t), lambda i: (scored(i)[1], 0, 0, 0)),
            pl.BlockSpec((4, C_HEAD_DIM), lambda i: (0, 0)),
            pl.BlockSpec((1, dh2), lambda i: (0, 0)),
        ],
        out_specs=pl.BlockSpec((None, t, dh2), o_map),
        out_shape=jax.ShapeDtypeStruct((b, s, C_HEADS * dh2), BF16),
        scratch_shapes=[scores, scores, maxima, maxima],
        compiler_params=_params(1),
        name="diff_attn",
    )(qkv, qkv, qkv, band, lam_qk, subln_g)


def _out_proj_kernel(a_ref, w_ref, x_ref, g_post_ref, g_next_ref, xo_ref, ho_ref):
    m = jnp.dot(a_ref[...], w_ref[...], preferred_element_type=F32)
    _residual_epilogue(m, x_ref, g_post_ref, g_next_ref, xo_ref, ho_ref)


def _out_projection(a, w, o, x, g_post, g_next):
    n, k = a.shape
    d = x.shape[-1]
    row = lambda width: pl.BlockSpec((TM, width), lambda i: (i, 0))
    vec = pl.BlockSpec((1, d), lambda i: (0, 0))
    return pl.pallas_call(
        _out_proj_kernel,
        grid=(n // TM,),
        in_specs=[row(k), pl.BlockSpec((None, k, d), lambda i: (o, 0, 0)), row(d), vec, vec],
        out_specs=[row(d), row(d)],
        out_shape=[jax.ShapeDtypeStruct((n, d), F32), jax.ShapeDtypeStruct((n, d), BF16)],
        compiler_params=_params(1),
        name="out_proj",
    )(a, w, x, g_post, g_next)


def _ffn_kernel(h_ref, wg_ref, wu_ref, wd_ref, x_hbm, g_post_ref, g_next_ref, xo_ref, *rest,
                emit_next):
    if emit_next:
        ho_hbm, x_buf, ho_buf, sems = rest
    else:
        x_buf, sems = rest
    i, f = pl.program_id(0), pl.program_id(1)
    last_tile, last_f = pl.num_programs(0) - 1, pl.num_programs(1) - 1
    rows = pl.ds(pl.multiple_of(i * x_buf.shape[0], x_buf.shape[0]), x_buf.shape[0])
    x_copy = pltpu.make_async_copy(x_hbm.at[rows], x_buf, sems.at[0])

    @pl.when(f == 0)
    def _():
        x_copy.start()
        xo_ref[...] = jnp.zeros(xo_ref.shape, F32)

    h = h_ref[...]
    gate = jnp.dot(h, wg_ref[...], preferred_element_type=F32)
    up = jnp.dot(h, wu_ref[...], preferred_element_type=F32)
    act = (_silu(gate) * up).astype(h.dtype)
    half = xo_ref.shape[1] // 2
    for cols in (slice(0, half), slice(half, 2 * half)):
        xo_ref[:, cols] += jnp.dot(act, wd_ref[:, cols], preferred_element_type=F32)

    @pl.when(f == last_f)
    def _():
        x_copy.wait()
        if not emit_next:
            _residual_epilogue(xo_ref[...], x_buf, g_post_ref, g_next_ref, xo_ref, None)
            return
        ho_copy = pltpu.make_async_copy(ho_buf, ho_hbm.at[rows], sems.at[1])

        @pl.when(i > 0)
        def _():
            ho_copy.wait()

        _residual_epilogue(xo_ref[...], x_buf, g_post_ref, g_next_ref, xo_ref, ho_buf)
        ho_copy.start()

        @pl.when(i == last_tile)
        def _():
            ho_copy.wait()


def _ffn(h, wg, wu, wd, layer, x, g_post, g_next):
    n, d = h.shape
    ff = wg.shape[-1]
    emit_next = g_next is not None
    row = pl.BlockSpec((TM_FFN, d), lambda i, f: (i, 0))
    vec = pl.BlockSpec((1, d), lambda i, f: (0, 0))
    hbm = pl.BlockSpec(memory_space=pl.ANY)
    out_specs = [row]
    out_shape = [jax.ShapeDtypeStruct((n, d), F32)]
    scratch = [pltpu.VMEM((TM_FFN, d), F32)]
    if emit_next:
        out_specs.append(hbm)
        out_shape.append(jax.ShapeDtypeStruct((n, d), BF16))
        scratch.append(pltpu.VMEM((TM_FFN, d), BF16))
    outs = pl.pallas_call(
        functools.partial(_ffn_kernel, emit_next=emit_next),
        grid=(n // TM_FFN, ff // TF),
        in_specs=[row,
                  pl.BlockSpec((None, d, TF), lambda i, f: (layer, 0, f)),
                  pl.BlockSpec((None, d, TF), lambda i, f: (layer, 0, f)),
                  pl.BlockSpec((None, TF, d), lambda i, f: (layer, f, 0)),
                  hbm, vec, vec],
        out_specs=out_specs,
        out_shape=out_shape,
        scratch_shapes=scratch + [pltpu.SemaphoreType.DMA((2,))],
        compiler_params=_params(2, FFN_VMEM_LIMIT_BYTES),
        name="ffn",
    )(h, wg, wu, wd, x, g_post, g_next if emit_next else g_post)
    return outs if emit_next else (outs[0], None)


def kernel(x, norm_g, w_in_ab, dw_kernel, conv_ln_g, conv_ln_b, sgu_ln_g, sgu_ln_b, w_spatial,
           b_spatial, w_out_ab, w_qkv_c, lambda_qk, subln_g, w_out_c, rel_bias, w_gate, w_up,
           w_down):
    b, s, d = x.shape
    n = b * s
    depth = norm_g.shape[0]
    c = conv_ln_g.shape[-1]
    gd = c // SGU_GROUPS

    w_in = w_in_ab.astype(BF16)
    w_out_e = w_out_ab.astype(BF16)
    w_qkv = w_qkv_c.astype(BF16)
    w_out_o = w_out_c.astype(BF16)
    wsp = w_spatial.astype(BF16)
    wg, wu, wd = w_gate.astype(BF16), w_up.astype(BF16), w_down.astype(BF16)
    band = _bias_band(rel_bias)

    x2 = x.reshape(n, d)
    h = _prenorm(x2, norm_g[0, 0][None, :])
    for layer in range(depth):
        g_post = norm_g[layer, 1][None, :]
        g_ffn = norm_g[layer, 2][None, :]
        if layer % 2 == 0:
            e = layer // 2
            a, zu, zv = _in_projection(h, w_in, e, sgu_ln_g[e][None, :], sgu_ln_b[e][None, :])
            bsp = jnp.repeat(b_spatial[e].T, gd, axis=1)
            taps = jnp.repeat(dw_kernel[e], SUBLANES, axis=0)
            x3, h3 = _even_mix(a.reshape(b, s, c), zu.reshape(b, s, c), zv.reshape(b, s, c),
                               x2.reshape(b, s, d), taps, conv_ln_g[e][None, :],
                               conv_ln_b[e][None, :], wsp[e], bsp, w_out_e, e, g_post, g_ffn)
            x2, h = x3.reshape(n, d), h3.reshape(n, d)
        else:
            o = layer // 2
            lambda_init = 0.8 - 0.6 * math.exp(-0.3 * layer)
            qkv = _qkv_projection(h, w_qkv, o)
            att = _diff_attention(qkv.reshape(b, s, -1), band, lambda_qk[o],
                                  subln_g[o][None, :], lambda_init)
            x2, h = _out_projection(att.reshape(n, -1), w_out_o, o, x2, g_post, g_ffn)
        g_next = norm_g[layer + 1, 0][None, :] if layer + 1 < depth else None
        x2, h = _ffn(h, wg, wu, wd, layer, x2, norm_g[layer, 3][None, :], g_next)
    return x2.reshape(b, s, d)
```

```python
import functools
import math

import jax
import jax.numpy as jnp
from jax import lax
from jax.experimental import pallas as pl
from jax.experimental.pallas import tpu as pltpu

F32 = jnp.float32
BF16 = jnp.bfloat16

EPS = 1e-6
CONV_WIDTH = 31
CHUNK = 128
SGU_GROUPS = 8
C_HEADS = 8
C_HEAD_DIM = 128
NUM_BUCKETS = 32
MAX_DISTANCE = 128
LOG2E = math.log2(math.e)

VMEM_LIMIT_BYTES = 56 * 1024 * 1024
FFN_VMEM_LIMIT_BYTES = 60 * 1024 * 1024
SUBLANES = 8
HALO = 16

TM = 512
TM_IN = 1024
TM_FFN = 1024
TF = 512
CAST_ROWS = 16
TS = 256
TQ = 256
CONV_ROWS = 32
FINISH_LEAD = 2
BAND_TILES = 5


def _params(grid_rank, vmem_limit_bytes=VMEM_LIMIT_BYTES):
    return pltpu.CompilerParams(dimension_semantics=("arbitrary",) * grid_rank,
                                vmem_limit_bytes=vmem_limit_bytes)


def _rms(v, g):
    return (v * lax.rsqrt(jnp.mean(v * v, axis=-1, keepdims=True) + EPS)) * g


def _layer_norm(v, g, b):
    mu = jnp.mean(v, axis=-1, keepdims=True)
    vc = v - mu
    return (vc * lax.rsqrt(jnp.mean(vc * vc, axis=-1, keepdims=True) + EPS)) * g + b


def _gelu(v):
    return 0.5 * v * (1.0 + lax.erf(v * math.sqrt(0.5)))


def _silu(v):
    return v * jax.nn.sigmoid(v)


def _residual_epilogue(m, x_ref, g_post_ref, g_next_ref, xo_ref, ho_ref):
    xn = x_ref[...] + _rms(m, g_post_ref[...])
    xo_ref[...] = xn
    if ho_ref is not None:
        ho_ref[...] = _rms(xn, g_next_ref[...]).astype(ho_ref.dtype)


def _prenorm_kernel(x_ref, g_ref, h_ref):
    h_ref[...] = _rms(x_ref[...], g_ref[...]).astype(h_ref.dtype)


def _prenorm(x2, g):
    n, d = x2.shape
    return pl.pallas_call(
        _prenorm_kernel,
        grid=(n // TM,),
        in_specs=[pl.BlockSpec((TM, d), lambda i: (i, 0)),
                  pl.BlockSpec((1, d), lambda i: (0, 0))],
        out_specs=pl.BlockSpec((TM, d), lambda i: (i, 0)),
        out_shape=jax.ShapeDtypeStruct((n, d), BF16),
        compiler_params=_params(1),
        name="prenorm",
    )(x2, g)


def _glu_in_kernel(h_ref, w_ref, a_ref):
    z = jnp.dot(h_ref[...], w_ref[...], preferred_element_type=F32)
    c = a_ref.shape[-1]
    a_ref[...] = z[:, :c] * jax.nn.sigmoid(z[:, c:])


def _sgu_in_kernel(h_ref, w_ref, g_ref, b_ref, zu_ref, zv_ref):
    z = jnp.dot(h_ref[...], w_ref[...], preferred_element_type=F32)
    c = zu_ref.shape[-1]
    zu_ref[...] = _gelu(z[:, :c]).astype(zu_ref.dtype)
    zv_ref[...] = _layer_norm(_gelu(z[:, c:]), g_ref[...], b_ref[...]).astype(zv_ref.dtype)


def _in_projection(h, w_in, e, sg, sb):
    n, d = h.shape
    c = w_in.shape[-1] // 4
    h_spec = pl.BlockSpec((TM_IN, d), lambda i: (i, 0))
    vec_spec = pl.BlockSpec((1, c), lambda i: (0, 0))
    out_spec = pl.BlockSpec((TM_IN, c), lambda i: (i, 0))
    a = pl.pallas_call(
        _glu_in_kernel,
        grid=(n // TM_IN,),
        in_specs=[h_spec, pl.BlockSpec((None, d, 2 * c), lambda i: (e, 0, 0))],
        out_specs=out_spec,
        out_shape=jax.ShapeDtypeStruct((n, c), F32),
        compiler_params=_params(1),
        name="glu_in",
    )(h, w_in)
    zu, zv = pl.pallas_call(
        _sgu_in_kernel,
        grid=(n // TM_IN,),
        in_specs=[h_spec, pl.BlockSpec((None, d, 2 * c), lambda i: (e, 0, 1)), vec_spec, vec_spec],
        out_specs=[out_spec, out_spec],
        out_shape=[jax.ShapeDtypeStruct((n, c), BF16)] * 2,
        compiler_params=_params(1),
        name="sgu_in",
    )(h, w_in, sg, sb)
    return a, zu, zv


def _even_mix_kernel(ap_ref, a_ref, an_ref, zu_ref, zv_ref, x_ref, dw_ref, cg_ref, cb_ref,
                     wsp_ref, bsp_ref, wo_ref, g_post_ref, g_next_ref,
                     xo_ref, ho_ref, win_ref, shift_ref, conv_ref, cat_ref):
    i = pl.program_id(1)
    ts, c = a_ref.shape
    win_ref[0:HALO, :] = jnp.where(i > 0, ap_ref[...], 0.0)
    win_ref[HALO:HALO + ts, :] = a_ref[...]
    win_ref[HALO + ts:, :] = jnp.where(i < pl.num_programs(1) - 1, an_ref[...], 0.0)
    span = shift_ref.shape[1]
    for r in range(1, SUBLANES):
        shift_ref[r - 1] = win_ref[r:r + span, :]
    first = HALO - CONV_WIDTH // 2
    for r0 in range(0, ts, CONV_ROWS):
        acc = jnp.zeros((CONV_ROWS // SUBLANES, SUBLANES, c), F32)
        for w in range(CONV_WIDTH):
            q, r = divmod(first + w, SUBLANES)
            rows = slice(r0 + q * SUBLANES, r0 + q * SUBLANES + CONV_ROWS)
            tap = win_ref[rows, :] if r == 0 else shift_ref[r - 1, rows, :]
            taps = dw_ref[w * SUBLANES:(w + 1) * SUBLANES, :]
            acc = acc + tap.reshape(-1, SUBLANES, c) * taps[None]
        conv_ref[r0:r0 + CONV_ROWS, :] = acc.reshape(CONV_ROWS, c)
    cat_ref[:, :c] = _silu(_layer_norm(conv_ref[...], cg_ref[...], cb_ref[...])).astype(cat_ref.dtype)
    gd = c // SGU_GROUPS
    for r0 in range(0, ts, CHUNK):
        for g in range(SGU_GROUPS):
            cols = slice(g * gd, (g + 1) * gd)
            s = jnp.dot(wsp_ref[g], zv_ref[r0:r0 + CHUNK, cols], preferred_element_type=F32)
            s = s + bsp_ref[:, cols]
            gated = zu_ref[r0:r0 + CHUNK, cols].astype(F32) * s
            cat_ref[r0:r0 + CHUNK, c + g * gd:c + (g + 1) * gd] = gated.astype(cat_ref.dtype)
    m = jnp.dot(cat_ref[...], wo_ref[...], preferred_element_type=F32)
    _residual_epilogue(m, x_ref, g_post_ref, g_next_ref, xo_ref, ho_ref)


def _even_mix(a, zu, zv, x, dw, cg, cb, wsp, bsp, w_out, e, g_post, g_next):
    b, s, c = a.shape
    d = x.shape[-1]
    nblk = TS // HALO
    tile = lambda width: pl.BlockSpec((None, TS, width), lambda bi, i: (bi, i, 0))
    const2 = lambda shape: pl.BlockSpec(shape, lambda bi, i: (0, 0))
    return pl.pallas_call(
        _even_mix_kernel,
        grid=(b, s // TS),
        in_specs=[
            pl.BlockSpec((None, HALO, c), lambda bi, i: (bi, jnp.maximum(i * nblk - 1, 0), 0)),
            tile(c),
            pl.BlockSpec((None, HALO, c),
                         lambda bi, i: (bi, jnp.minimum((i + 1) * nblk, s // HALO - 1), 0)),
            tile(c), tile(c), tile(d),
            const2((CONV_WIDTH * SUBLANES, c)), const2((1, c)), const2((1, c)),
            pl.BlockSpec((SGU_GROUPS, CHUNK, CHUNK), lambda bi, i: (0, 0, 0)),
            const2((CHUNK, c)),
            pl.BlockSpec((None, 2 * c, d), lambda bi, i: (e, 0, 0)),
            const2((1, d)), const2((1, d)),
        ],
        out_specs=[tile(d), tile(d)],
        out_shape=[jax.ShapeDtypeStruct((b, s, d), F32), jax.ShapeDtypeStruct((b, s, d), BF16)],
        scratch_shapes=[pltpu.VMEM((TS + 2 * HALO, c), F32),
                        pltpu.VMEM((SUBLANES - 1, TS + 2 * HALO - SUBLANES, c), F32),
                        pltpu.VMEM((TS, c), F32), pltpu.VMEM((TS, 2 * c), BF16)],
        compiler_params=_params(2),
        name="even_mix",
    )(a, a, a, zu, zv, x, dw, cg, cb, wsp, bsp, w_out, g_post, g_next)


def _qkv_kernel(h_ref, w_ref, o_ref, *, q_scale):
    z = jnp.dot(h_ref[...], w_ref[...], preferred_element_type=F32)
    scale = jnp.where(pl.program_id(0) == 0, q_scale, 1.0)
    o_ref[...] = (z * scale).astype(o_ref.dtype)


def _qkv_projection(h, w_qkv, o):
    n, d = h.shape
    width = w_qkv.shape[-1] // 3
    return pl.pallas_call(
        functools.partial(_qkv_kernel, q_scale=C_HEAD_DIM ** -0.5 * LOG2E),
        grid=(3, n // TM_IN),
        in_specs=[pl.BlockSpec((TM_IN, d), lambda j, i: (i, 0)),
                  pl.BlockSpec((None, d, width), lambda j, i: (o, 0, j))],
        out_specs=pl.BlockSpec((TM_IN, width), lambda j, i: (i, j)),
        out_shape=jax.ShapeDtypeStruct((n, 3 * width), BF16),
        compiler_params=_params(2),
        name="qkv_proj",
    )(h, w_qkv)


def _t5_bucket(rel):
    nb = NUM_BUCKETS // 2
    max_exact = nb // 2
    ret = jnp.where(rel > 0, nb, 0)
    n = jnp.abs(rel)
    nf = jnp.maximum(n, 1).astype(jnp.float32)
    large = max_exact + (jnp.log(nf / max_exact) / math.log(MAX_DISTANCE / max_exact)
                         * (nb - max_exact)).astype(jnp.int32)
    large = jnp.minimum(large, nb - 1)
    return ret + jnp.where(n < max_exact, n, large)


def _bias_band_kernel(bucket_ref, rb_ref, band_ref):
    h = pl.program_id(0)
    t = band_ref.shape[-1]
    width = (BAND_TILES + 1) * t
    bucket = bucket_ref[...]
    row = jnp.zeros(bucket.shape, F32)
    for bkt in range(NUM_BUCKETS):
        row = jnp.where(bucket == bkt, rb_ref[bkt, h] * LOG2E, row)
    rolled = pltpu.roll(jnp.broadcast_to(row, (t, width)), 0, 1, stride=1, stride_axis=0)
    for d in range(BAND_TILES):
        band_ref[d] = rolled[:, (d + 1) * t:(d + 2) * t]


def _bias_band(rel_bias):
    t = TQ
    assert t >= MAX_DISTANCE
    half = (BAND_TILES + 1) * t // 2
    bucket = _t5_bucket(jnp.arange(-half, half, dtype=jnp.int32))[None, :]
    return pl.pallas_call(
        _bias_band_kernel,
        grid=(C_HEADS,),
        in_specs=[pl.BlockSpec((1, 2 * half), lambda h: (0, 0)),
                  pl.BlockSpec(memory_space=pltpu.SMEM)],
        out_specs=pl.BlockSpec((None, BAND_TILES, t, t), lambda h: (h, 0, 0, 0)),
        out_shape=jax.ShapeDtypeStruct((C_HEADS, BAND_TILES, t, t), F32),
        compiler_params=_params(1),
        name="bias_band",
    )(bucket, rel_bias)


def _attn_step(qi, q_ref, k_ref, v_ref, band_ref, lam_ref, sg_ref, o_ref,
               s_new, m_new, s_old, m_old, lambda_init):
    t, dh2 = q_ref.shape
    dh = dh2 // 2
    half = t // 2
    maps = range(2)
    mx = [None, None]
    ls = [None, None]
    acc = [None, None]
    nk = k_ref.shape[0] // t

    def finish(kj):
        v = v_ref[kj * t:(kj + 1) * t, :]
        for c in maps:
            e_lo = jnp.exp2(s_old[c, kj, :, :half] - m_old[c])
            e_hi = jnp.exp2(s_old[c, kj, :, half:] - m_old[c])
            ls[c] = e_lo + e_hi if ls[c] is None else ls[c] + (e_lo + e_hi)
            p = jnp.concatenate([e_lo.astype(v.dtype), e_hi.astype(v.dtype)], axis=1)
            pv = jnp.dot(p, v, preferred_element_type=F32)
            acc[c] = pv if acc[c] is None else acc[c] + pv

    def score(kj):
        bias = band_ref[jnp.clip(kj - qi, -2, 2) + 2]
        for c in maps:
            q = q_ref[:, c * dh:(c + 1) * dh]
            k = k_ref[kj * t:(kj + 1) * t, c * dh:(c + 1) * dh]
            s = lax.dot_general(q, k, (((1,), (1,)), ((), ())), preferred_element_type=F32)
            s = s + bias
            s_new[c, kj] = s
            part = jnp.maximum(s[:, :half], s[:, half:])
            mx[c] = part if mx[c] is None else jnp.maximum(mx[c], part)

    for kj in range(nk + FINISH_LEAD):
        if kj < nk:
            finish(kj)
        if kj >= FINISH_LEAD:
            score(kj - FINISH_LEAD)
    for c in maps:
        m_new[c] = jnp.broadcast_to(jnp.max(mx[c], axis=1, keepdims=True), (t, half))
    outs = [acc[c] / jnp.sum(ls[c], axis=1, keepdims=True) for c in maps]
    lq = lam_ref[...]
    lam = (jnp.exp(jnp.sum(lq[0:1] * lq[1:2], axis=1, keepdims=True))
           - jnp.exp(jnp.sum(lq[2:3] * lq[3:4], axis=1, keepdims=True)) + lambda_init)
    o = outs[0] - lam * outs[1]
    o_ref[...] = (_rms(o, sg_ref[...]) * (1.0 - lambda_init)).astype(o_ref.dtype)


def _diff_attn_kernel(q_ref, k_ref, v_ref, band_ref, lam_ref, sg_ref, o_ref,
                      s_even, s_odd, m_even, m_odd, *, lambda_init, tiles_per_head):
    i = pl.program_id(0)
    tile = jnp.minimum(i, pl.num_programs(0) - 2)
    qi = tile % tiles_per_head
    step = functools.partial(_attn_step, qi, q_ref, k_ref, v_ref, band_ref, lam_ref, sg_ref, o_ref)

    @pl.when(i == 0)
    def _():
        s_odd[...] = jnp.zeros(s_odd.shape, F32)
        m_odd[...] = jnp.zeros(m_odd.shape, F32)

    @pl.when(i % 2 == 0)
    def _():
        step(s_even, m_even, s_odd, m_odd, lambda_init)

    @pl.when(i % 2 == 1)
    def _():
        step(s_odd, m_odd, s_even, m_even, lambda_init)


def _diff_attention(qkv, band, lam_qk, subln_g, lambda_init):
    b, s, _ = qkv.shape
    dh2 = 2 * C_HEAD_DIM
    t = TQ
    nq = s // t
    tiles = b * C_HEADS * nq

    def scored(i):
        tile = jnp.minimum(i, tiles - 1)
        return tile // (C_HEADS * nq), (tile // nq) % C_HEADS, tile % nq

    def finished(i):
        tile = jnp.maximum(i - 1, 0)
        return tile // (C_HEADS * nq), (tile // nq) % C_HEADS, tile % nq

    def q_map(i):
        bi, h, qi = scored(i)
        return bi, qi, h

    def k_map(i):
        bi, h, _ = scored(i)
        return bi, 0, C_HEADS + h

    def v_map(i):
        bi, h, _ = finished(i)
        return bi, 0, 2 * C_HEADS + h

    def o_map(i):
        bi, h, qi = finished(i)
        return bi, qi, h

    scores = pltpu.VMEM((2, nq, t, t), F32)
    maxima = pltpu.VMEM((2, t, t // 2), F32)
    return pl.pallas_call(
        functools.partial(_diff_attn_kernel, lambda_init=lambda_init, tiles_per_head=nq),
        grid=(tiles + 1,),
        in_specs=[
            pl.BlockSpec((None, t, dh2), q_map),
            pl.BlockSpec((None, s, dh2), k_map),
            pl.BlockSpec((None, s, dh2), v_map),
            pl.BlockSpec((None, BAND_TILES, t, t), lambda i: (scored(i)[1], 0, 0, 0)),
            pl.BlockSpec((4, C_HEAD_DIM), lambda i: (0, 0)),
            pl.BlockSpec((1, dh2), lambda i: (0, 0)),
        ],
        out_specs=pl.BlockSpec((None, t, dh2), o_map),
        out_shape=jax.ShapeDtypeStruct((b, s, C_HEADS * dh2), BF16),
        scratch_shapes=[scores, scores, maxima, maxima],
        compiler_params=_params(1),
        name="diff_attn",
    )(qkv, qkv, qkv, band, lam_qk, subln_g)


def _out_proj_kernel(a_ref, w_ref, x_ref, g_post_ref, g_next_ref, xo_ref, ho_ref):
    m = jnp.dot(a_ref[...], w_ref[...], preferred_element_type=F32)
    _residual_epilogue(m, x_ref, g_post_ref, g_next_ref, xo_ref, ho_ref)


def _out_projection(a, w, o, x, g_post, g_next):
    n, k = a.shape
    d = x.shape[-1]
    row = lambda width: pl.BlockSpec((TM, width), lambda i: (i, 0))
    vec = pl.BlockSpec((1, d), lambda i: (0, 0))
    return pl.pallas_call(
        _out_proj_kernel,
        grid=(n // TM,),
        in_specs=[row(k), pl.BlockSpec((None, k, d), lambda i: (o, 0, 0)), row(d), vec, vec],
        out_specs=[row(d), row(d)],
        out_shape=[jax.ShapeDtypeStruct((n, d), F32), jax.ShapeDtypeStruct((n, d), BF16)],
        compiler_params=_params(1),
        name="out_proj",
    )(a, w, x, g_post, g_next)


def _ffn_kernel(h_ref, wg_ref, wu_ref, wd_ref, x_hbm, g_post_ref, g_next_ref, *rest,
                has_next, next_layer, up_slabs):
    if has_next:
        (wgf_hbm, wuf_hbm, wdf_hbm, xo_ref, ho_hbm, wgb_hbm, wub_hbm, wdb_hbm,
         x_buf, ho_buf, fg_buf, fu_buf, fd_buf, bg_buf, bu_buf, bd_buf, sems) = rest
    else:
        xo_ref, x_buf, sems = rest
    i, f = pl.program_id(0), pl.program_id(1)
    last_tile, last_f = pl.num_programs(0) - 1, pl.num_programs(1) - 1
    step = i * pl.num_programs(1) + f
    last_step = pl.num_programs(0) * pl.num_programs(1) - 1
    rows = pl.ds(pl.multiple_of(i * x_buf.shape[0], x_buf.shape[0]), x_buf.shape[0])
    x_copy = pltpu.make_async_copy(x_hbm.at[rows], x_buf, sems.at[0])

    def slab_copies(f32_hbm, bf16_hbm, f32_buf, bf16_buf, sem):
        n_rows = f32_buf.shape[0]
        slab = pl.ds(pl.multiple_of(step * n_rows, n_rows), n_rows)
        return (pltpu.make_async_copy(f32_hbm.at[next_layer, slab], f32_buf, sems.at[sem]),
                pltpu.make_async_copy(bf16_buf, bf16_hbm.at[slab], sems.at[sem + 1]))

    def cast_slab(read, write, f32_buf, bf16_buf, final):
        read.wait()

        @pl.when(step > 0)
        def _():
            write.wait()

        bf16_buf[...] = f32_buf[...].astype(bf16_buf.dtype)
        write.start()

        @pl.when(final)
        def _():
            write.wait()

    if has_next:
        gate_copies = slab_copies(wgf_hbm, wgb_hbm, fg_buf, bg_buf, 2)
        up_copies = slab_copies(wuf_hbm, wub_hbm, fu_buf, bu_buf, 4)
        down_copies = slab_copies(wdf_hbm, wdb_hbm, fd_buf, bd_buf, 6)
        down_copies[0].start()

        @pl.when(step < up_slabs)
        def _():
            gate_copies[0].start()
            up_copies[0].start()

    @pl.when(f == 0)
    def _():
        x_copy.start()
        xo_ref[...] = jnp.zeros(xo_ref.shape, F32)

    h = h_ref[...]
    gate = jnp.dot(h, wg_ref[...], preferred_element_type=F32)
    up = jnp.dot(h, wu_ref[...], preferred_element_type=F32)
    act = (_silu(gate) * up).astype(h.dtype)
    half = xo_ref.shape[1] // 2
    for cols in (slice(0, half), slice(half, 2 * half)):
        xo_ref[:, cols] += jnp.dot(act, wd_ref[:, cols], preferred_element_type=F32)

    if has_next:
        cast_slab(*down_copies, fd_buf, bd_buf, step == last_step)

        @pl.when(step < up_slabs)
        def _():
            cast_slab(*gate_copies, fg_buf, bg_buf, step == up_slabs - 1)
            cast_slab(*up_copies, fu_buf, bu_buf, step == up_slabs - 1)

    @pl.when(f == last_f)
    def _():
        x_copy.wait()
        if not has_next:
            _residual_epilogue(xo_ref[...], x_buf, g_post_ref, g_next_ref, xo_ref, None)
            return
        ho_copy = pltpu.make_async_copy(ho_buf, ho_hbm.at[rows], sems.at[1])

        @pl.when(i > 0)
        def _():
            ho_copy.wait()

        _residual_epilogue(xo_ref[...], x_buf, g_post_ref, g_next_ref, xo_ref, ho_buf)
        ho_copy.start()

        @pl.when(i == last_tile)
        def _():
            ho_copy.wait()


def _ffn(h, wg, wu, wd, x, g_post, g_next, next_f32, next_layer):
    n, d = h.shape
    ff = wg.shape[-1]
    has_next = next_f32 is not None
    steps = (n // TM_FFN) * (ff // TF)
    up_slabs = d // CAST_ROWS
    down_rows = ff // steps
    assert up_slabs <= steps and down_rows * steps == ff and down_rows % CAST_ROWS == 0
    row = pl.BlockSpec((TM_FFN, d), lambda i, f: (i, 0))
    vec = pl.BlockSpec((1, d), lambda i, f: (0, 0))
    hbm = pl.BlockSpec(memory_space=pl.ANY)
    in_specs = [row,
                pl.BlockSpec((d, TF), lambda i, f: (0, f)),
                pl.BlockSpec((d, TF), lambda i, f: (0, f)),
                pl.BlockSpec((TF, d), lambda i, f: (f, 0)),
                hbm, vec, vec]
    out_specs = [row]
    out_shape = [jax.ShapeDtypeStruct((n, d), F32)]
    scratch = [pltpu.VMEM((TM_FFN, d), F32)]
    args = [h, wg, wu, wd, x, g_post, g_next if has_next else g_post]
    n_sems = 2
    if has_next:
        in_specs += [hbm] * 3
        args += list(next_f32)
        out_specs += [hbm] * 4
        out_shape += [jax.ShapeDtypeStruct((n, d), BF16), jax.ShapeDtypeStruct((d, ff), BF16),
                      jax.ShapeDtypeStruct((d, ff), BF16), jax.ShapeDtypeStruct((ff, d), BF16)]
        scratch += [pltpu.VMEM((TM_FFN, d), BF16),
                    pltpu.VMEM((CAST_ROWS, ff), F32), pltpu.VMEM((CAST_ROWS, ff), F32),
                    pltpu.VMEM((down_rows, d), F32),
                    pltpu.VMEM((CAST_ROWS, ff), BF16), pltpu.VMEM((CAST_ROWS, ff), BF16),
                    pltpu.VMEM((down_rows, d), BF16)]
        n_sems = 8
    outs = pl.pallas_call(
        functools.partial(_ffn_kernel, has_next=has_next, next_layer=next_layer, up_slabs=up_slabs),
        grid=(n // TM_FFN, ff // TF),
        in_specs=in_specs,
        out_specs=out_specs,
        out_shape=out_shape,
        scratch_shapes=scratch + [pltpu.SemaphoreType.DMA((n_sems,))],
        compiler_params=_params(2, FFN_VMEM_LIMIT_BYTES),
        name="ffn",
    )(*args)
    if has_next:
        return outs[0], outs[1], tuple(outs[2:])
    return outs[0], None, None


def kernel(x, norm_g, w_in_ab, dw_kernel, conv_ln_g, conv_ln_b, sgu_ln_g, sgu_ln_b, w_spatial,
           b_spatial, w_out_ab, w_qkv_c, lambda_qk, subln_g, w_out_c, rel_bias, w_gate, w_up,
           w_down):
    b, s, d = x.shape
    n = b * s
    depth = norm_g.shape[0]
    c = conv_ln_g.shape[-1]
    gd = c // SGU_GROUPS

    w_in = w_in_ab.astype(BF16)
    w_out_e = w_out_ab.astype(BF16)
    w_qkv = w_qkv_c.astype(BF16)
    w_out_o = w_out_c.astype(BF16)
    wsp = w_spatial.astype(BF16)
    ffn_w = (w_gate[0].astype(BF16), w_up[0].astype(BF16), w_down[0].astype(BF16))
    band = _bias_band(rel_bias)

    x2 = x.reshape(n, d)
    h = _prenorm(x2, norm_g[0, 0][None, :])
    for layer in range(depth):
        g_post = norm_g[layer, 1][None, :]
        g_ffn = norm_g[layer, 2][None, :]
        if layer % 2 == 0:
            e = layer // 2
            a, zu, zv = _in_projection(h, w_in, e, sgu_ln_g[e][None, :], sgu_ln_b[e][None, :])
            bsp = jnp.repeat(b_spatial[e].T, gd, axis=1)
            taps = jnp.repeat(dw_kernel[e], SUBLANES, axis=0)
            x3, h3 = _even_mix(a.reshape(b, s, c), zu.reshape(b, s, c), zv.reshape(b, s, c),
                               x2.reshape(b, s, d), taps, conv_ln_g[e][None, :],
                               conv_ln_b[e][None, :], wsp[e], bsp, w_out_e, e, g_post, g_ffn)
            x2, h = x3.reshape(n, d), h3.reshape(n, d)
        else:
            o = layer // 2
            lambda_init = 0.8 - 0.6 * math.exp(-0.3 * layer)
            qkv = _qkv_projection(h, w_qkv, o)
            att = _diff_attention(qkv.reshape(b, s, -1), band, lambda_qk[o],
                                  subln_g[o][None, :], lambda_init)
            x2, h = _out_projection(att.reshape(n, -1), w_out_o, o, x2, g_post, g_ffn)
        last = layer + 1 == depth
        g_next = None if last else norm_g[layer + 1, 0][None, :]
        x2, h, ffn_w = _ffn(h, *ffn_w, x2, norm_g[layer, 3][None, :], g_next,
                            None if last else (w_gate, w_up, w_down), layer + 1)
    return x2.reshape(b, s, d)
```
